```python
import jax
import jax.numpy as jnp
from jax import lax
import numpy as np

D_MODEL = 2048
BATCH = 2
SEQ = 4096
DEPTH = 4

N_MIXERS = 4
N_META = 16
EPS = 1e-6
D_FF = 256 * ((8 * D_MODEL // 3 + 255) // 256)

MLSTM_HEADS = 8
MLSTM_DV = D_MODEL // MLSTM_HEADS
MLSTM_DK = MLSTM_DV // 2
MLSTM_CHUNK = 64
MLSTM_QK_W = MLSTM_HEADS * MLSTM_DK
MLSTM_V_W = MLSTM_HEADS * MLSTM_DV
MLSTM_IN = 2 * MLSTM_QK_W + 2 * MLSTM_V_W + 2 * MLSTM_HEADS

POOL_WINDOWS = (2, 4, 8, 16)
POOL_GROUP = D_MODEL // len(POOL_WINDOWS)

GDN_DK = 128
GDN_DV = 128
GDN_QK_HEADS = D_MODEL // GDN_DK
GDN_V_HEADS = 2 * GDN_QK_HEADS
GDN_CONV = 4
GDN_CHUNK = 64
GDN_QK_W = GDN_QK_HEADS * GDN_DK
GDN_V_W = GDN_V_HEADS * GDN_DV
GDN_CONV_CH = 2 * GDN_QK_W + GDN_V_W
GDN_IN = GDN_CONV_CH + GDN_V_W + 2 * GDN_V_HEADS

SWA_DH = 64
SWA_HQ = D_MODEL // SWA_DH
SWA_GROUP = 8
SWA_HKV = SWA_HQ // SWA_GROUP
SWA_WINDOW = 128
SWA_BLOCK = SWA_WINDOW
SWA_Q_W = SWA_HQ * SWA_DH
SWA_KV_W = SWA_HKV * SWA_DH
SWA_IN = SWA_Q_W + 2 * SWA_KV_W
ROPE_THETA = 10000.0

kernel_name = 'hybrid_interleaved_mlstm_pool_gdn_swa'


def _n_layers_of(mixer):
    return len(range(mixer, DEPTH, N_MIXERS))


def _rms_normalize(x):
    xf = x.astype(jnp.float32)
    return xf * lax.rsqrt(jnp.mean(xf * xf, axis=-1, keepdims=True) + EPS)


def _rmsnorm(x, w):
    return (_rms_normalize(x) * w.astype(jnp.float32)).astype(x.dtype)


def _swiglu(x, w_gate, w_up, w_down):
    return (jax.nn.silu(x @ w_gate) * (x @ w_up)) @ w_down


def _to_chunks(t, c):
    t = t.reshape(t.shape[0], t.shape[1], t.shape[2] // c, c, *t.shape[3:])
    return jnp.moveaxis(t, 2, 0)


def _from_chunks(t):
    t = jnp.moveaxis(t, 0, 2)
    return t.reshape(t.shape[0], t.shape[1], -1, *t.shape[4:])


def _meta_then_chunks(step, state, seqs, chunk):
    meta = tuple(s[:, :, :N_META] for s in seqs)
    real = tuple(_to_chunks(s[:, :, N_META:], chunk) for s in seqs)
    state, out_meta = step(state, meta)
    _, out_real = lax.scan(step, state, real)
    return jnp.concatenate([out_meta, _from_chunks(out_real)], axis=2)


def _mlstm_chunk(state, xs):
    c_st, n_st, m_st = state
    q, k, v, li, lf = xs
    c = q.shape[2]
    causal = jnp.tril(jnp.ones((c, c), bool))
    b = jnp.cumsum(lf, axis=-1)
    log_w = jnp.where(causal, b[..., :, None] - b[..., None, :] + li[..., None, :], -jnp.inf)
    log_init = b + m_st[..., None]
    m_t = jnp.maximum(log_init, jnp.max(log_w, axis=-1))
    w = jnp.exp(log_w - m_t[..., None])
    w_init = jnp.exp(log_init - m_t)
    qk = jnp.einsum('bhtd,bhsd->bhts', q, k) * w
    num = w_init[..., None] * jnp.einsum('bhtd,bhde->bhte', q, c_st) + jnp.einsum('bhts,bhse->bhte', qk, v)
    den = w_init * jnp.einsum('bhtd,bhd->bht', q, n_st) + jnp.sum(qk, axis=-1)
    h = num / jnp.maximum(jnp.abs(den), jnp.exp(-m_t))[..., None]
    log_end_init = b[..., -1] + m_st
    log_end = b[..., -1:] - b + li
    m_new = jnp.maximum(log_end_init, jnp.max(log_end, axis=-1))
    a_init = jnp.exp(log_end_init - m_new)
    a = jnp.exp(log_end - m_new[..., None])
    c_new = a_init[..., None, None] * c_st + jnp.einsum('bhs,bhsd,bhse->bhde', a, k, v)
    n_new = a_init[..., None] * n_st + jnp.einsum('bhs,bhsd->bhd', a, k)
    return (c_new, n_new, m_new), h


def _mlstm(u, w_in, b_if, norm_w, w_out):
    bsz, L, _ = u.shape
    p = u @ w_in
    q, k, v, og, gates = jnp.split(p, [MLSTM_QK_W, 2 * MLSTM_QK_W, 2 * MLSTM_QK_W + MLSTM_V_W,
                                       2 * MLSTM_QK_W + 2 * MLSTM_V_W], axis=-1)
    heads = lambda t, d: t.reshape(bsz, L, MLSTM_HEADS, d).transpose(0, 2, 1, 3).astype(jnp.float32)
    q = heads(q, MLSTM_DK) * (MLSTM_DK ** -0.5)
    k = heads(k, MLSTM_DK)
    v = heads(v, MLSTM_DV)
    gates = (gates.astype(jnp.float32) + b_if.astype(jnp.float32)).transpose(0, 2, 1)
    li = gates[:, :MLSTM_HEADS]
    lf = jax.nn.log_sigmoid(gates[:, MLSTM_HEADS:])
    state0 = (jnp.zeros((bsz, MLSTM_HEADS, MLSTM_DK, MLSTM_DV), jnp.float32),
              jnp.zeros((bsz, MLSTM_HEADS, MLSTM_DK), jnp.float32),
              jnp.zeros((bsz, MLSTM_HEADS), jnp.float32))
    hh = _meta_then_chunks(_mlstm_chunk, state0, (q, k, v, li, lf), MLSTM_CHUNK)
    hh = _rms_normalize(hh.transpose(0, 2, 1, 3)).reshape(bsz, L, MLSTM_V_W)
    hh = hh * norm_w.astype(jnp.float32) * jax.nn.sigmoid(og.astype(jnp.float32))
    return hh.astype(u.dtype) @ w_out


def _pool_mixer(u, w_group, scale):
    bsz, L, _ = u.shape
    uf = u.astype(jnp.float32)
    cs = jnp.cumsum(uf, axis=1)
    count = jnp.arange(L) + 1
    outs = []
    for gi, win in enumerate(POOL_WINDOWS):
        lo, hi = gi * POOL_GROUP, (gi + 1) * POOL_GROUP
        c = cs[..., lo:hi]
        lag = jnp.pad(c[:, :L - win], ((0, 0), (win, 0), (0, 0)))
        mean = (c - lag) / jnp.minimum(count, win).astype(jnp.float32)[None, :, None]
        outs.append(mean - uf[..., lo:hi])
    pooled = jnp.stack(outs, axis=2).astype(u.dtype)
    y = jnp.einsum('blgc,gcd->blgd', pooled, w_group).reshape(bsz, L, D_MODEL)
    return y * scale


def _causal_depthwise_conv(x, w):
    return lax.conv_general_dilated(x, w[:, None, :], window_strides=(1,), padding=[(w.shape[0] - 1, 0)],
                                    dimension_numbers=('NWC', 'WIO', 'NWC'), feature_group_count=x.shape[-1])


def _l2norm(x):
    return x * lax.rsqrt(jnp.sum(x * x, axis=-1, keepdims=True) + EPS)


def _gdn_chunk(s_st, xs):
    q, k, v, g, beta = xs
    c = q.shape[2]
    causal = jnp.tril(jnp.ones((c, c), bool))
    strict = jnp.tril(jnp.ones((c, c), bool), -1)
    gc = jnp.cumsum(g, axis=-1)
    decay = jnp.exp(jnp.where(causal, gc[..., :, None] - gc[..., None, :], -jnp.inf))
    kb = k * beta[..., None]
    lower = jnp.where(strict, jnp.einsum('bhtd,bhsd->bhts', kb, k) * decay, 0.0)
    a_mat = lower + jnp.eye(c, dtype=lower.dtype)
    rhs = jnp.concatenate([v * beta[..., None], kb * jnp.exp(gc)[..., None]], axis=-1)
    sol = lax.linalg.triangular_solve(a_mat, rhs, left_side=True, lower=True, unit_diagonal=True)
    u_vec, w_vec = sol[..., :GDN_DV], sol[..., GDN_DV:]
    v_new = u_vec - jnp.einsum('bhtd,bhde->bhte', w_vec, s_st)
    attn = jnp.einsum('bhtd,bhsd->bhts', q, k) * decay
    o = jnp.einsum('bhtd,bhde->bhte', q * jnp.exp(gc)[..., None], s_st) + jnp.einsum('bhts,bhse->bhte', attn, v_new)
    g_last = gc[..., -1]
    s_new = jnp.exp(g_last)[..., None, None] * s_st + jnp.einsum(
        'bhsd,bhse->bhde', k * jnp.exp(g_last[..., None] - gc)[..., None], v_new)
    return s_new, o


def _gated_deltanet(u, w_in, conv_w, a_log, dt_bias, norm_w, w_out):
    bsz, L, _ = u.shape
    p = u @ w_in
    qkv, z, b_pre, a_pre = jnp.split(p, [GDN_CONV_CH, GDN_CONV_CH + GDN_V_W, GDN_CONV_CH + GDN_V_W + GDN_V_HEADS], axis=-1)
    qkv = jax.nn.silu(_causal_depthwise_conv(qkv, conv_w))
    q, k, v = jnp.split(qkv, [GDN_QK_W, 2 * GDN_QK_W], axis=-1)
    heads = lambda t, n, d: t.reshape(bsz, L, n, d).transpose(0, 2, 1, 3).astype(jnp.float32)
    rep = GDN_V_HEADS // GDN_QK_HEADS
    q = jnp.repeat(_l2norm(heads(q, GDN_QK_HEADS, GDN_DK)) * (GDN_DK ** -0.5), rep, axis=1)
    k = jnp.repeat(_l2norm(heads(k, GDN_QK_HEADS, GDN_DK)), rep, axis=1)
    v = heads(v, GDN_V_HEADS, GDN_DV)
    beta = jax.nn.sigmoid(b_pre.astype(jnp.float32)).transpose(0, 2, 1)
    g = (-jnp.exp(a_log.astype(jnp.float32))
         * jax.nn.softplus(a_pre.astype(jnp.float32) + dt_bias.astype(jnp.float32))).transpose(0, 2, 1)
    s0 = jnp.zeros((bsz, GDN_V_HEADS, GDN_DK, GDN_DV), jnp.float32)
    o = _meta_then_chunks(_gdn_chunk, s0, (q, k, v, g, beta), GDN_CHUNK)
    o = _rms_normalize(o.transpose(0, 2, 1, 3)) * norm_w.astype(jnp.float32)
    o = o * jax.nn.silu(z.astype(jnp.float32).reshape(bsz, L, GDN_V_HEADS, GDN_DV))
    return o.reshape(bsz, L, GDN_V_W).astype(u.dtype) @ w_out


def _rope_tables(L, d):
    inv = ROPE_THETA ** (-jnp.arange(0, d, 2, dtype=jnp.float32) / d)
    ang = jnp.arange(L, dtype=jnp.float32)[:, None] * inv[None, :]
    ang = jnp.concatenate([ang, ang], axis=-1)
    return jnp.cos(ang), jnp.sin(ang)


def _rope(x, cos, sin):
    shape = (1, x.shape[1]) + (1,) * (x.ndim - 3) + (x.shape[-1],)
    cos, sin = cos.reshape(shape), sin.reshape(shape)
    x1, x2 = jnp.split(x, 2, axis=-1)
    return x * cos + jnp.concatenate([-x2, x1], axis=-1) * sin


def _swa_sinks(u, w_qkv, b_qkv, sinks, w_out, b_out):
    bsz, L, _ = u.shape
    p = (u @ w_qkv + b_qkv).astype(jnp.float32)
    q, k, v = jnp.split(p, [SWA_Q_W, SWA_Q_W + SWA_KV_W], axis=-1)
    q = q.reshape(bsz, L, SWA_HKV, SWA_GROUP, SWA_DH)
    k = k.reshape(bsz, L, SWA_HKV, SWA_DH)
    v = v.reshape(bsz, L, SWA_HKV, SWA_DH)
    cos, sin = _rope_tables(L, SWA_DH)
    q, k = _rope(q, cos, sin), _rope(k, cos, sin)
    nb = -(-L // SWA_BLOCK)
    lp = nb * SWA_BLOCK
    pad_end = lambda t: jnp.pad(t, ((0, 0), (0, lp - L)) + ((0, 0),) * (t.ndim - 2))
    q, k, v = pad_end(q), pad_end(k), pad_end(v)
    qb = q.reshape(bsz, nb, SWA_BLOCK, SWA_HKV, SWA_GROUP, SWA_DH)

    def band(t):
        tp = jnp.pad(t, ((0, 0), (SWA_BLOCK, 0), (0, 0), (0, 0)))
        prev = tp[:, :lp].reshape(bsz, nb, SWA_BLOCK, SWA_HKV, SWA_DH)
        cur = tp[:, SWA_BLOCK:].reshape(bsz, nb, SWA_BLOCK, SWA_HKV, SWA_DH)
        return jnp.concatenate([prev, cur], axis=2)

    kb, vb = band(k), band(v)
    s = jnp.einsum('bnqhgd,bnkhd->bnhgqk', qb, kb) * (SWA_DH ** -0.5)
    blk = jnp.arange(nb)[:, None, None] * SWA_BLOCK
    qpos = blk + jnp.arange(SWA_BLOCK)[None, :, None]
    kpos = blk - SWA_BLOCK + jnp.arange(2 * SWA_BLOCK)[None, None, :]
    mask = (kpos <= qpos) & (qpos - kpos < SWA_WINDOW) & (kpos >= 0)
    s = jnp.where(mask[None, :, None, None], s, -jnp.inf)
    sink = jnp.broadcast_to(sinks.astype(jnp.float32).reshape(1, 1, SWA_HKV, SWA_GROUP, 1, 1), s.shape[:-1] + (1,))
    prob = jax.nn.softmax(jnp.concatenate([s, sink], axis=-1), axis=-1)[..., :-1]
    o = jnp.einsum('bnhgqk,bnkhd->bnqhgd', prob, vb).reshape(bsz, lp, SWA_Q_W)[:, :L]
    return o.astype(u.dtype) @ w_out + b_out


def setup_inputs(seed: int = 0) -> dict:
    key = jax.random.key(seed)
    ks = iter(jax.random.split(key, 40))
    nrm = lambda shape, s=1.0: s * jax.random.normal(next(ks), shape, jnp.float32)
    dense = lambda shape: nrm(shape, shape[-2] ** -0.5)
    gain = lambda shape: 1.0 + nrm(shape, 0.02)
    na, nb, nc, nd = (_n_layers_of(m) for m in range(N_MIXERS))
    x = nrm((BATCH, SEQ, D_MODEL))
    meta_tokens = nrm((N_META, D_MODEL))
    norm_w = gain((DEPTH, 3, D_MODEL))
    ffn_w_gate = dense((DEPTH, 2, D_MODEL, D_FF))
    ffn_w_up = dense((DEPTH, 2, D_MODEL, D_FF))
    ffn_w_down = dense((DEPTH, 2, D_FF, D_MODEL))
    mlstm_w_in = dense((na, D_MODEL, MLSTM_IN))
    b_i = nrm((na, MLSTM_HEADS), 0.1)
    b_f = jnp.linspace(3.0, 6.0, MLSTM_HEADS, dtype=jnp.float32)[None, :] + nrm((na, MLSTM_HEADS), 0.1)
    mlstm_b_if = jnp.concatenate([b_i, b_f], axis=-1)
    mlstm_norm_w = gain((na, MLSTM_V_W))
    mlstm_w_out = dense((na, MLSTM_V_W, D_MODEL))
    pool_w = dense((nb, len(POOL_WINDOWS), POOL_GROUP, POOL_GROUP))
    pool_scale = gain((nb, D_MODEL))
    gdn_w_in = dense((nc, D_MODEL, GDN_IN))
    gdn_conv_w = nrm((nc, GDN_CONV, GDN_CONV_CH), GDN_CONV ** -0.5)
    gdn_a_log = jnp.log(jax.random.uniform(next(ks), (nc, GDN_V_HEADS), jnp.float32, 1.0, 16.0))
    dt = jnp.exp(jax.random.uniform(next(ks), (nc, GDN_V_HEADS), jnp.float32,
                                    float(np.log(1e-3)), float(np.log(1e-1))))
    gdn_dt_bias = dt + jnp.log(-jnp.expm1(-dt))
    gdn_norm_w = gain((nc, GDN_DV))
    gdn_w_out = dense((nc, GDN_V_W, D_MODEL))
    swa_w_qkv = dense((nd, D_MODEL, SWA_IN))
    swa_b_qkv = nrm((nd, SWA_IN), 0.02)
    swa_sinks = nrm((nd, SWA_HQ), 0.5)
    swa_w_out = dense((nd, SWA_Q_W, D_MODEL))
    swa_b_out = nrm((nd, D_MODEL), 0.02)
    final_norm_w = gain((D_MODEL,))
    return {'x': x, 'meta_tokens': meta_tokens, 'norm_w': norm_w,
            'ffn_w_gate': ffn_w_gate, 'ffn_w_up': ffn_w_up, 'ffn_w_down': ffn_w_down,
            'mlstm_w_in': mlstm_w_in, 'mlstm_b_if': mlstm_b_if, 'mlstm_norm_w': mlstm_norm_w, 'mlstm_w_out': mlstm_w_out,
            'pool_w': pool_w, 'pool_scale': pool_scale,
            'gdn_w_in': gdn_w_in, 'gdn_conv_w': gdn_conv_w, 'gdn_a_log': gdn_a_log, 'gdn_dt_bias': gdn_dt_bias,
            'gdn_norm_w': gdn_norm_w, 'gdn_w_out': gdn_w_out,
            'swa_w_qkv': swa_w_qkv, 'swa_b_qkv': swa_b_qkv, 'swa_sinks': swa_sinks, 'swa_w_out': swa_w_out,
            'swa_b_out': swa_b_out, 'final_norm_w': final_norm_w}


def reference(x, meta_tokens, norm_w, ffn_w_gate, ffn_w_up, ffn_w_down,
              mlstm_w_in, mlstm_b_if, mlstm_norm_w, mlstm_w_out,
              pool_w, pool_scale,
              gdn_w_in, gdn_conv_w, gdn_a_log, gdn_dt_bias, gdn_norm_w, gdn_w_out,
              swa_w_qkv, swa_b_qkv, swa_sinks, swa_w_out, swa_b_out, final_norm_w):
    bsz = x.shape[0]
    meta = jnp.broadcast_to(meta_tokens.astype(x.dtype)[None], (bsz, N_META, D_MODEL))
    h = jnp.concatenate([meta, x], axis=1)
    for i in range(DEPTH):
        m, j = i % N_MIXERS, i // N_MIXERS
        h = h + 0.5 * _swiglu(_rmsnorm(h, norm_w[i, 0]), ffn_w_gate[i, 0], ffn_w_up[i, 0], ffn_w_down[i, 0])
        u = _rmsnorm(h, norm_w[i, 1])
        if m == 0:
            y = _mlstm(u, mlstm_w_in[j], mlstm_b_if[j], mlstm_norm_w[j], mlstm_w_out[j])
        elif m == 1:
            y = _pool_mixer(u, pool_w[j], pool_scale[j])
        elif m == 2:
            y = _gated_deltanet(u, gdn_w_in[j], gdn_conv_w[j], gdn_a_log[j], gdn_dt_bias[j], gdn_norm_w[j], gdn_w_out[j])
        else:
            y = _swa_sinks(u, swa_w_qkv[j], swa_b_qkv[j], swa_sinks[j], swa_w_out[j], swa_b_out[j])
        h = h + y
        h = h + 0.5 * _swiglu(_rmsnorm(h, norm_w[i, 2]), ffn_w_gate[i, 1], ffn_w_up[i, 1], ffn_w_down[i, 1])
    return _rmsnorm(h, final_norm_w)[:, N_META:]
```

```python
import functools
import math

import jax
import jax.numpy as jnp
from jax import lax
from jax.experimental import pallas as pl
from jax.experimental.pallas import tpu as pltpu

F32 = jnp.float32
BF16 = jnp.bfloat16

D_MODEL = 2048
DEPTH = 4
N_MIXERS = 4
N_META = 16
EPS = 1e-6

MLSTM_HEADS = 8
MLSTM_DV = 256
MLSTM_DK = 128
MLSTM_QK_W = MLSTM_HEADS * MLSTM_DK
MLSTM_V_W = MLSTM_HEADS * MLSTM_DV
MLSTM_MAIN_W = 2 * MLSTM_QK_W + 2 * MLSTM_V_W

POOL_WINDOWS = (2, 4, 8, 16)
POOL_GROUP = D_MODEL // len(POOL_WINDOWS)
POOL_HALO = 16

GDN_DK = 128
GDN_DV = 128
GDN_QK_HEADS = 16
GDN_V_HEADS = 32
GDN_CONV = 4
GDN_QK_W = GDN_QK_HEADS * GDN_DK
GDN_V_W = GDN_V_HEADS * GDN_DV
GDN_CONV_CH = 2 * GDN_QK_W + GDN_V_W
GDN_MAIN_W = GDN_CONV_CH + GDN_V_W

SWA_DH = 64
SWA_HQ = 32
SWA_GROUP = 8
SWA_HKV = 4
SWA_WINDOW = 128
SWA_Q_W = SWA_HQ * SWA_DH
SWA_KV_W = SWA_HKV * SWA_DH
SWA_IN = SWA_Q_W + 2 * SWA_KV_W
ROPE_THETA = 10000.0

LEAD_PAD = 48
ROW0 = LEAD_PAD + N_META
SEQ_BLOCK = 128
LANES = 128

VMEM_LIMIT_BYTES = 56 * 1024 * 1024


def _rows_per_batch(seq):
    return -(-(ROW0 + seq) // SEQ_BLOCK) * SEQ_BLOCK


def _divisor_tile(total, target, multiple):
    best = None
    for t in range(multiple, min(total, target) + 1, multiple):
        if total % t == 0:
            best = t
    assert best is not None, (total, target, multiple)
    return best


def _params(*semantics):
    return pltpu.CompilerParams(dimension_semantics=semantics, vmem_limit_bytes=VMEM_LIMIT_BYTES)


def _rms_normalize(x):
    return x * lax.rsqrt(jnp.mean(x * x, axis=-1, keepdims=True) + EPS)


def _sigmoid(x):
    return 1.0 / (1.0 + jnp.exp(-x))


def _softplus(x):
    return jnp.maximum(x, 0.0) + jnp.log(1.0 + jnp.exp(-jnp.abs(x)))


def _dot(a, b):
    return jnp.dot(a, b, preferred_element_type=F32)


def _dot_nt(a, b):
    return lax.dot_general(a, b, (((1,), (1,)), ((), ())), preferred_element_type=F32)


def _dot_tn(a, b):
    return lax.dot_general(a, b, (((0,), (0,)), ((), ())), preferred_element_type=F32)


def _cumsum_rows(x):
    n = x.shape[0]
    row = lax.broadcasted_iota(jnp.int32, x.shape, 0)
    shift = 1
    while shift < n:
        x = x + jnp.where(row >= shift, pltpu.roll(x, shift, 0), 0.0)
        shift *= 2
    return x


def _column_to_row(col, eye):
    return jnp.sum(jnp.where(eye, col, 0.0), axis=0, keepdims=True)


def _ffn_kernel(h_ref, nw_ref, wg_ref, wu_ref, wd_ref, fw_ref, o_ref, xn_ref, *, final_norm):
    j = pl.program_id(1)

    @pl.when(j == 0)
    def _():
        x = h_ref[...]
        xn_ref[...] = (_rms_normalize(x) * nw_ref[...]).astype(BF16)
        o_ref[...] = x

    xn = xn_ref[...]
    g = _dot(xn, wg_ref[...].astype(BF16))
    u = _dot(xn, wu_ref[...].astype(BF16))
    a = (0.5 * g * _sigmoid(g) * u).astype(BF16)
    o_ref[...] += _dot(a, wd_ref[...].astype(BF16))

    if final_norm:
        @pl.when(j == pl.num_programs(1) - 1)
        def _():
            o_ref[...] = _rms_normalize(o_ref[...]) * fw_ref[...]


def _ffn(h, nw, w_gate, w_up, w_down, layer, half, final_w=None, *, tm_target=768, tf=256):
    rows, d = h.shape
    d_ff = w_gate.shape[-1]
    tm = _divisor_tile(rows, tm_target, 16)
    assert d_ff % tf == 0
    final_norm = final_w is not None
    fw = final_w if final_norm else nw
    kern = functools.partial(_ffn_kernel, final_norm=final_norm)
    return pl.pallas_call(
        kern,
        grid=(rows // tm, d_ff // tf),
        in_specs=[
            pl.BlockSpec((tm, d), lambda i, j: (i, 0)),
            pl.BlockSpec((1, d), lambda i, j: (0, 0)),
            pl.BlockSpec((None, None, d, tf), lambda i, j: (layer, half, 0, j)),
            pl.BlockSpec((None, None, d, tf), lambda i, j: (layer, half, 0, j)),
            pl.BlockSpec((None, None, tf, d), lambda i, j: (layer, half, j, 0)),
            pl.BlockSpec((1, d), lambda i, j: (0, 0)),
        ],
        out_specs=pl.BlockSpec((tm, d), lambda i, j: (i, 0)),
        out_shape=jax.ShapeDtypeStruct((rows, d), F32),
        scratch_shapes=[pltpu.VMEM((tm, d), BF16)],
        compiler_params=_params("parallel", "arbitrary"),
        name="ffn",
    )(h, nw.reshape(1, d), w_gate, w_up, w_down, fw.reshape(1, d))


def _norm_proj_kernel(*refs, has_side):
    if has_side:
        h_ref, nw_ref, w_ref, b_ref, ws_ref, bs_ref, o_ref, os_ref, xn_ref = refs
    else:
        h_ref, nw_ref, w_ref, b_ref, o_ref, xn_ref = refs
    j = pl.program_id(1)

    @pl.when(j == 0)
    def _():
        xn = (_rms_normalize(h_ref[...]) * nw_ref[...]).astype(BF16)
        xn_ref[...] = xn
        if has_side:
            os_ref[...] = _dot(xn, ws_ref[...].astype(BF16)) + bs_ref[...]

    o_ref[...] = (_dot(xn_ref[...], w_ref[...].astype(BF16)) + b_ref[...]).astype(o_ref.dtype)


def _norm_proj(h, nw, w, bias, n_main, side=None, *, tm_target=1056, tn=512):
    rows, d = h.shape
    tm = _divisor_tile(rows, tm_target, 16)
    assert n_main % tn == 0
    in_specs = [
        pl.BlockSpec((tm, d), lambda i, j: (i, 0)),
        pl.BlockSpec((1, d), lambda i, j: (0, 0)),
        pl.BlockSpec((d, tn), lambda i, j: (0, j)),
        pl.BlockSpec((1, tn), lambda i, j: (0, j)),
    ]
    out_specs = [pl.BlockSpec((tm, tn), lambda i, j: (i, j))]
    out_shape = [jax.ShapeDtypeStruct((rows, n_main), BF16)]
    args = [h, nw.reshape(1, d), w, bias.reshape(1, -1)]
    if side is not None:
        w_side, b_side = side
        n_side = w_side.shape[1]
        in_specs += [pl.BlockSpec((d, n_side), lambda i, j: (0, 0)), pl.BlockSpec((1, n_side), lambda i, j: (0, 0))]
        out_specs += [pl.BlockSpec((tm, n_side), lambda i, j: (i, 0))]
        out_shape += [jax.ShapeDtypeStruct((rows, n_side), F32)]
        args += [w_side, b_side.reshape(1, n_side)]
    return pl.pallas_call(
        functools.partial(_norm_proj_kernel, has_side=side is not None),
        grid=(rows // tm, n_main // tn),
        in_specs=in_specs,
        out_specs=out_specs,
        out_shape=out_shape,
        scratch_shapes=[pltpu.VMEM((tm, d), BF16)],
        compiler_params=_params("parallel", "arbitrary"),
        name="norm_proj",
    )(*args)


def _proj_residual_kernel(a_ref, w_ref, b_ref, h_ref, o_ref, *, tm, rows_per_batch):
    i = pl.program_id(0)
    y = _dot(a_ref[...], w_ref[...].astype(BF16)) + b_ref[...]
    row = lax.rem(i * tm, rows_per_batch) + lax.broadcasted_iota(jnp.int32, (tm, 1), 0)
    o_ref[...] = h_ref[...] + jnp.where(row >= LEAD_PAD, y, 0.0)


def _proj_residual(a, w, bias, h, rows_per_batch, *, tm_target=1056, tn=512):
    rows, k = a.shape
    d = h.shape[1]
    tm = _divisor_tile(rows_per_batch, tm_target, 16)
    kern = functools.partial(_proj_residual_kernel, tm=tm, rows_per_batch=rows_per_batch)
    return pl.pallas_call(
        kern,
        grid=(rows // tm, d // tn),
        in_specs=[
            pl.BlockSpec((tm, k), lambda i, j: (i, 0)),
            pl.BlockSpec((k, tn), lambda i, j: (0, j)),
            pl.BlockSpec((1, tn), lambda i, j: (0, j)),
            pl.BlockSpec((tm, tn), lambda i, j: (i, j)),
        ],
        out_specs=pl.BlockSpec((tm, tn), lambda i, j: (i, j)),
        out_shape=jax.ShapeDtypeStruct((rows, d), F32),
        compiler_params=_params("parallel", "arbitrary"),
        name="proj_residual",
    )(a, w, bias.reshape(1, d), h)


def _mlstm_kernel(q_ref, k_ref, v_ref, og_ref, g_ref, nw_ref, o_ref, c_ref, n_ref, m_ref, *, blk):
    c = pl.program_id(1)

    @pl.when(c == 0)
    def _():
        c_ref[...] = jnp.zeros_like(c_ref)
        n_ref[...] = jnp.zeros_like(n_ref)
        m_ref[...] = jnp.zeros_like(m_ref)

    gates = g_ref[...]
    row = c * blk + lax.broadcasted_iota(jnp.int32, (blk, 1), 0)
    valid = row >= LEAD_PAD
    log_f = jnp.where(valid, jnp.minimum(gates, 0.0) - jnp.log(1.0 + jnp.exp(-jnp.abs(gates))), 0.0)
    b_all = _cumsum_rows(log_f)
    li_all = jnp.where(valid, gates, -jnp.inf)

    ri = lax.broadcasted_iota(jnp.int32, (blk, blk), 0)
    ci = lax.broadcasted_iota(jnp.int32, (blk, blk), 1)
    eye = ri == ci
    causal = ci <= ri
    scale = MLSTM_DK ** -0.5

    for hd in range(MLSTM_HEADS):
        li_c = li_all[:, hd:hd + 1]
        b_c = b_all[:, MLSTM_HEADS + hd:MLSTM_HEADS + hd + 1]
        d_c = li_c - b_c
        d_r = _column_to_row(d_c, eye)
        m_prev = m_ref[hd][:, :1]
        log_w = jnp.where(causal, b_c + d_r, -jnp.inf)
        log_init = b_c + m_prev
        m_t = jnp.maximum(log_init, jnp.max(log_w, axis=-1, keepdims=True))
        w = jnp.exp(log_w - m_t) * scale
        w_init = jnp.exp(log_init - m_t) * scale

        qh = q_ref[:, hd * MLSTM_DK:(hd + 1) * MLSTM_DK]
        kh = k_ref[:, hd * MLSTM_DK:(hd + 1) * MLSTM_DK]
        vh = v_ref[:, hd * MLSTM_DV:(hd + 1) * MLSTM_DV]
        c_st = c_ref[hd]
        n_st = n_ref[hd]
        qk = _dot_nt(qh, kh) * w
        num = w_init * _dot(qh, c_st.astype(BF16)) + _dot(qk.astype(BF16), vh)
        qn = jnp.sum(qh.astype(F32) * n_st, axis=-1, keepdims=True)
        den = w_init * qn + jnp.sum(qk, axis=-1, keepdims=True)
        hh = num / jnp.maximum(jnp.abs(den), jnp.exp(-m_t))

        b_last = b_c[blk - 1:blk, :]
        log_end_init = b_last + m_prev
        log_end = b_last + d_c
        m_new = jnp.maximum(log_end_init, jnp.max(log_end, axis=0, keepdims=True))
        a_init = jnp.exp(log_end_init - m_new)
        ka = kh.astype(F32) * jnp.exp(log_end - m_new)
        c_ref[hd] = a_init * c_st + _dot_tn(ka.astype(BF16), vh)
        n_ref[hd] = a_init * n_st + jnp.sum(ka, axis=0, keepdims=True)
        m_ref[hd] = jnp.broadcast_to(m_new, (1, LANES))

        sl = slice(hd * MLSTM_DV, (hd + 1) * MLSTM_DV)
        gate = _sigmoid(og_ref[:, sl].astype(F32))
        o_ref[:, sl] = (_rms_normalize(hh) * nw_ref[:, sl] * gate).astype(o_ref.dtype)


def _mlstm_core(p_main, gates, norm_w, batch, rows_per_batch):
    blk = SEQ_BLOCK
    nblk = rows_per_batch // blk
    rows = p_main.shape[0]
    idx = lambda col: (lambda b, c: (b * nblk + c, col))
    kern = functools.partial(_mlstm_kernel, blk=blk)
    return pl.pallas_call(
        kern,
        grid=(batch, nblk),
        in_specs=[
            pl.BlockSpec((blk, MLSTM_QK_W), idx(0)),
            pl.BlockSpec((blk, MLSTM_QK_W), idx(1)),
            pl.BlockSpec((blk, MLSTM_V_W), idx(1)),
            pl.BlockSpec((blk, MLSTM_V_W), idx(2)),
            pl.BlockSpec((blk, 2 * MLSTM_HEADS), idx(0)),
            pl.BlockSpec((1, MLSTM_V_W), lambda b, c: (0, 0)),
        ],
        out_specs=pl.BlockSpec((blk, MLSTM_V_W), idx(0)),
        out_shape=jax.ShapeDtypeStruct((rows, MLSTM_V_W), BF16),
        scratch_shapes=[
            pltpu.VMEM((MLSTM_HEADS, MLSTM_DK, MLSTM_DV), F32),
            pltpu.VMEM((MLSTM_HEADS, 1, MLSTM_DK), F32),
            pltpu.VMEM((MLSTM_HEADS, 1, LANES), F32),
        ],
        compiler_params=_params("parallel", "arbitrary"),
        name="mlstm_core",
    )(p_main, p_main, p_main, p_main, gates, norm_w.reshape(1, MLSTM_V_W))


def _pool_kernel(h_ref, halo_ref, nw_ref, w_ref, sc_ref, o_ref, ext_ref, *, tm, rows_per_batch):
    i = pl.program_id(0)
    x = h_ref[...]
    nw = nw_ref[...]
    u = _rms_normalize(x) * nw
    ext_ref[0:POOL_HALO, :] = _rms_normalize(halo_ref[...]) * nw
    ext_ref[POOL_HALO:, :] = u
    pos = lax.rem(i * tm, rows_per_batch) + lax.broadcasted_iota(jnp.int32, (tm, 1), 0) - LEAD_PAD
    valid = pos >= 0
    for gi, win in enumerate(POOL_WINDOWS):
        sl = slice(gi * POOL_GROUP, (gi + 1) * POOL_GROUP)
        terms = [ext_ref[POOL_HALO - s:POOL_HALO - s + tm, sl] for s in range(win)]
        while len(terms) > 1:
            terms = [terms[a] + terms[a + 1] for a in range(0, len(terms), 2)]
        count = jnp.clip(pos + 1, 1, win).astype(F32)
        pooled = terms[0] / count - u[:, sl]
        y = _dot(pooled.astype(BF16), w_ref[gi].astype(BF16)) * sc_ref[:, sl]
        o_ref[:, sl] = x[:, sl] + jnp.where(valid, y, 0.0)


def _pool_mixer(h, nw, w_group, scale, rows_per_batch, *, tm_target=384):
    rows, d = h.shape
    tm = _divisor_tile(rows_per_batch, tm_target, POOL_HALO)
    kern = functools.partial(_pool_kernel, tm=tm, rows_per_batch=rows_per_batch)
    halo_blocks = tm // POOL_HALO
    return pl.pallas_call(
        kern,
        grid=(rows // tm,),
        in_specs=[
            pl.BlockSpec((tm, d), lambda i: (i, 0)),
            pl.BlockSpec((POOL_HALO, d), lambda i: (jnp.maximum(i * halo_blocks - 1, 0), 0)),
            pl.BlockSpec((1, d), lambda i: (0, 0)),
            pl.BlockSpec(w_group.shape, lambda i: (0, 0, 0)),
            pl.BlockSpec((1, d), lambda i: (0, 0)),
        ],
        out_specs=pl.BlockSpec((tm, d), lambda i: (i, 0)),
        out_shape=jax.ShapeDtypeStruct((rows, d), F32),
        scratch_shapes=[pltpu.VMEM((tm + POOL_HALO, d), F32)],
        compiler_params=_params("parallel"),
        name="pool_mixer",
    )(h, h, nw.reshape(1, d), w_group, scale.reshape(1, d))


GDN_CARRY = 8


def _unit_lower_inverse(strict_lower):
    n = strict_lower.shape[0]
    ri = lax.broadcasted_iota(jnp.int32, (n, n), 0)
    ci = lax.broadcasted_iota(jnp.int32, (n, n), 1)
    a = -strict_lower
    p = jnp.where(ri == ci, 1.0, 0.0) + a
    terms = 2
    while terms < n:
        ab = a.astype(BF16)
        a = _dot(ab, ab)
        p = p + _dot(p.astype(BF16), a.astype(BF16))
        terms *= 2
    return p


def _gdn_kernel(x_ref, z_ref, e_ref, cw_ref, al_ref, dt_ref, nw_ref, o_ref,
                ext_ref, s_ref, q_s, k_s, v_s, o_s, *, blk):
    c = pl.program_id(1)

    @pl.when(c == 0)
    def _():
        ext_ref[0:GDN_CARRY, :] = jnp.zeros((GDN_CARRY, GDN_CONV_CH), F32)
        s_ref[...] = jnp.zeros_like(s_ref)

    ext_ref[GDN_CARRY:, :] = x_ref[...].astype(F32)
    n_cols = GDN_CONV_CH // LANES
    for cb in range(n_cols):
        sl = slice(cb * LANES, (cb + 1) * LANES)
        acc = None
        for tap in range(GDN_CONV):
            off = GDN_CARRY - (GDN_CONV - 1) + tap
            term = ext_ref[off:off + blk, sl] * cw_ref[tap:tap + 1, sl]
            acc = term if acc is None else acc + term
        y = acc * _sigmoid(acc)
        if cb < 2 * GDN_QK_HEADS:
            y = y * lax.rsqrt(jnp.sum(y * y, axis=-1, keepdims=True) + EPS)
            if cb < GDN_QK_HEADS:
                q_s[cb] = (y * (GDN_DK ** -0.5)).astype(BF16)
            else:
                k_s[cb - GDN_QK_HEADS] = y.astype(BF16)
        else:
            v_s[cb - 2 * GDN_QK_HEADS] = y.astype(BF16)
    ext_ref[0:GDN_CARRY, :] = ext_ref[blk:blk + GDN_CARRY, :]

    row = c * blk + lax.broadcasted_iota(jnp.int32, (blk, 1), 0)
    valid = row >= LEAD_PAD
    e = e_ref[...]
    beta_all = jnp.where(valid, _sigmoid(e[:, :GDN_V_HEADS]), 0.0)
    g_all = jnp.where(valid, -jnp.exp(al_ref[...]) * _softplus(e[:, GDN_V_HEADS:] + dt_ref[...]), 0.0)
    gc_all = _cumsum_rows(g_all)

    ri = lax.broadcasted_iota(jnp.int32, (blk, blk), 0)
    ci = lax.broadcasted_iota(jnp.int32, (blk, blk), 1)
    eye = ri == ci
    causal = ci <= ri
    strict = ci < ri
    head_lane = lax.broadcasted_iota(jnp.int32, (1, GDN_V_HEADS), 1)

    def qk_head(j, carry):
        q = q_s[j]
        k = k_s[j]
        kk = _dot_nt(k, k)
        qk = _dot_nt(q, k)
        kf = k.astype(F32)
        for r in range(GDN_V_HEADS // GDN_QK_HEADS):
            jv = 2 * j + r
            pick = head_lane == jv
            gc_c = jnp.sum(jnp.where(pick, gc_all, 0.0), axis=-1, keepdims=True)
            beta_c = jnp.sum(jnp.where(pick, beta_all, 0.0), axis=-1, keepdims=True)
            gc_r = _column_to_row(gc_c, eye)
            decay = jnp.exp(jnp.where(causal, gc_c - gc_r, -jnp.inf))
            t_inv = _unit_lower_inverse(jnp.where(strict, kk * decay, 0.0) * beta_c)
            egc = jnp.exp(gc_c)
            v = v_s[jv]
            rhs = jnp.concatenate([v.astype(F32) * beta_c, kf * (beta_c * egc)], axis=-1).astype(BF16)
            sol = _dot(t_inv.astype(BF16), rhs)
            s_st = s_ref[jv]
            s_b = s_st.astype(BF16)
            v_new = sol[:, :GDN_DV] - _dot(sol[:, GDN_DV:].astype(BF16), s_b)
            o_s[jv] = _dot(q, s_b) * egc + _dot((qk * decay).astype(BF16), v_new.astype(BF16))
            g_last = gc_c[blk - 1:blk, :]
            v_dec = (v_new * jnp.exp(g_last - gc_c)).astype(BF16)
            s_ref[jv] = jnp.exp(g_last) * s_st + _dot_tn(k, v_dec)
        return carry

    lax.fori_loop(0, GDN_QK_HEADS, qk_head, 0)

    for jv in range(GDN_V_HEADS):
        sl = slice(jv * GDN_DV, (jv + 1) * GDN_DV)
        z = z_ref[:, sl].astype(F32)
        o_ref[:, sl] = (_rms_normalize(o_s[jv]) * nw_ref[...] * (z * _sigmoid(z))).astype(o_ref.dtype)


def _gdn_core(p_main, side, conv_w, a_log, dt_bias, norm_w, batch, rows_per_batch, *, blk=64):
    nblk = rows_per_batch // blk
    rows = p_main.shape[0]
    idx = lambda col: (lambda b, c: (b * nblk + c, col))
    const2 = lambda b, c: (0, 0)
    kern = functools.partial(_gdn_kernel, blk=blk)
    return pl.pallas_call(
        kern,
        grid=(batch, nblk),
        in_specs=[
            pl.BlockSpec((blk, GDN_CONV_CH), idx(0)),
            pl.BlockSpec((blk, GDN_V_W), idx(GDN_CONV_CH // GDN_V_W)),
            pl.BlockSpec((blk, 2 * GDN_V_HEADS), idx(0)),
            pl.BlockSpec((GDN_CONV, GDN_CONV_CH), const2),
            pl.BlockSpec((1, GDN_V_HEADS), const2),
            pl.BlockSpec((1, GDN_V_HEADS), const2),
            pl.BlockSpec((1, GDN_DV), const2),
        ],
        out_specs=pl.BlockSpec((blk, GDN_V_W), idx(0)),
        out_shape=jax.ShapeDtypeStruct((rows, GDN_V_W), BF16),
        scratch_shapes=[
            pltpu.VMEM((blk + GDN_CARRY, GDN_CONV_CH), F32),
            pltpu.VMEM((GDN_V_HEADS, GDN_DK, GDN_DV), F32),
            pltpu.VMEM((GDN_QK_HEADS, blk, GDN_DK), BF16),
            pltpu.VMEM((GDN_QK_HEADS, blk, GDN_DK), BF16),
            pltpu.VMEM((GDN_V_HEADS, blk, GDN_DV), BF16),
            pltpu.VMEM((GDN_V_HEADS, blk, GDN_DV), F32),
        ],
        compiler_params=_params("parallel", "arbitrary"),
        name="gdn_core",
    )(p_main, p_main, side, conv_w, a_log.reshape(1, -1), dt_bias.reshape(1, -1), norm_w.reshape(1, -1))


def _rope_table_kernel(cos_ref, sin_ref):
    shape = cos_ref.shape
    pos = (lax.broadcasted_iota(jnp.int32, shape, 0) - LEAD_PAD).astype(F32)
    pair = lax.rem(lax.broadcasted_iota(jnp.int32, shape, 1), SWA_DH // 2).astype(F32)
    inv = jnp.exp(pair * (-2.0 * math.log(ROPE_THETA) / SWA_DH))
    ang = pos * inv
    cos_ref[...] = jnp.cos(ang)
    sin_ref[...] = jnp.sin(ang)


def _rope_tables(rows_per_batch):
    shape = jax.ShapeDtypeStruct((rows_per_batch, LANES), F32)
    return pl.pallas_call(_rope_table_kernel, out_shape=[shape, shape], name="rope_tables")()


def _swa_kernel(sink_ref, q_ref, kv_ref, kvp_ref, cos_ref, sin_ref, cosp_ref, sinp_ref, o_ref):
    n = pl.program_id(1)
    blk = SEQ_BLOCK
    lane = lax.broadcasted_iota(jnp.int32, (1, LANES), 1)
    low = lane < SWA_DH
    first_half = lax.rem(lane, SWA_DH) < SWA_DH // 2

    def rope(x, cos, sin):
        rot = jnp.where(first_half, -pltpu.roll(x, LANES - SWA_DH // 2, 1), pltpu.roll(x, SWA_DH // 2, 1))
        return x * cos + rot * sin

    cos, sin = cos_ref[...], sin_ref[...]
    cosp, sinp = cosp_ref[...], sinp_ref[...]

    stacked = lax.broadcasted_iota(jnp.int32, (2 * blk, 2 * blk), 0)
    q_row = n * blk + jnp.where(stacked < blk, stacked, stacked - blk)
    k_row = (n - 1) * blk + lax.broadcasted_iota(jnp.int32, (2 * blk, 2 * blk), 1)
    mask = (k_row <= q_row) & (q_row - k_row < SWA_WINDOW) & (k_row >= LEAD_PAD)
    top = lax.broadcasted_iota(jnp.int32, (2 * blk, 1), 0) < blk

    def both_heads(x, head_in_pair):
        swapped = pltpu.roll(x, SWA_DH, 1)
        return jnp.where(low, x, swapped) if head_in_pair == 0 else jnp.where(low, swapped, x)

    for g in range(SWA_HKV):
        pair_col, head_in_pair = divmod(g, 2)
        ksl = slice(pair_col * LANES, (pair_col + 1) * LANES)
        vsl = slice(SWA_KV_W + pair_col * LANES, SWA_KV_W + (pair_col + 1) * LANES)
        k_cur = both_heads(rope(kv_ref[:, ksl].astype(F32), cos, sin), head_in_pair)
        k_prev = both_heads(rope(kvp_ref[:, ksl].astype(F32), cosp, sinp), head_in_pair)
        kk = jnp.concatenate([k_prev, k_cur], axis=0).astype(BF16)
        v_cur = both_heads(kv_ref[:, vsl].astype(F32), head_in_pair)
        v_prev = both_heads(kvp_ref[:, vsl].astype(F32), head_in_pair)
        v2 = jnp.concatenate([v_prev, v_cur], axis=0)
        v_stack = jnp.concatenate([jnp.where(low, v2, 0.0), jnp.where(low, 0.0, v2)], axis=0).astype(BF16)
        for p in range(SWA_GROUP // 2):
            col = g * (SWA_GROUP // 2) + p
            qsl = slice(col * LANES, (col + 1) * LANES)
            qp = rope(q_ref[:, qsl].astype(F32), cos, sin)
            q2 = jnp.concatenate([jnp.where(low, qp, 0.0), jnp.where(low, 0.0, qp)], axis=0).astype(BF16)
            s = _dot_nt(q2, kk) * (SWA_DH ** -0.5)
            s = jnp.where(mask, s, -jnp.inf)
            sink = jnp.where(top, sink_ref[2 * col], sink_ref[2 * col + 1])
            m = jnp.maximum(jnp.max(s, axis=-1, keepdims=True), sink)
            e = jnp.exp(s - m)
            prob = e / (jnp.sum(e, axis=-1, keepdims=True) + jnp.exp(sink - m))
            p2 = jnp.concatenate([prob[:blk], prob[blk:]], axis=1).astype(BF16)
            o_ref[:, qsl] = _dot(p2, v_stack).astype(o_ref.dtype)


def _swa_core(p_main, sinks, cos, sin, batch, rows_per_batch):
    blk = SEQ_BLOCK
    nblk = rows_per_batch // blk
    rows = p_main.shape[0]
    kv_col = SWA_Q_W // (2 * SWA_KV_W)
    cur = lambda col: (lambda b, n: (b * nblk + n, col))
    prev = lambda col: (lambda b, n: (jnp.maximum(b * nblk + n - 1, 0), col))
    tab_cur = lambda b, n: (n, 0)
    tab_prev = lambda b, n: (jnp.maximum(n - 1, 0), 0)
    return pl.pallas_call(
        _swa_kernel,
        grid=(batch, nblk),
        in_specs=[
            pl.BlockSpec(memory_space=pltpu.SMEM),
            pl.BlockSpec((blk, SWA_Q_W), cur(0)),
            pl.BlockSpec((blk, 2 * SWA_KV_W), cur(kv_col)),
            pl.BlockSpec((blk, 2 * SWA_KV_W), prev(kv_col)),
            pl.BlockSpec((blk, LANES), tab_cur),
            pl.BlockSpec((blk, LANES), tab_cur),
            pl.BlockSpec((blk, LANES), tab_prev),
            pl.BlockSpec((blk, LANES), tab_prev),
        ],
        out_specs=pl.BlockSpec((blk, SWA_Q_W), cur(0)),
        out_shape=jax.ShapeDtypeStruct((rows, SWA_Q_W), BF16),
        compiler_params=_params("parallel", "parallel"),
        name="swa_core",
    )(sinks, p_main, p_main, p_main, cos, sin, cos, sin)


def kernel(x, meta_tokens, norm_w, ffn_w_gate, ffn_w_up, ffn_w_down, mlstm_w_in, mlstm_b_if, mlstm_norm_w, mlstm_w_out, pool_w, pool_scale, gdn_w_in, gdn_conv_w, gdn_a_log, gdn_dt_bias, gdn_norm_w, gdn_w_out, swa_w_qkv, swa_b_qkv, swa_sinks, swa_w_out, swa_b_out, final_norm_w):
    batch, seq, d = x.shape
    rpb = _rows_per_batch(seq)
    meta = jnp.broadcast_to(meta_tokens.astype(x.dtype)[None], (batch, N_META, d))
    h = jnp.concatenate([jnp.zeros((batch, LEAD_PAD, d), x.dtype), meta, x,
                         jnp.zeros((batch, rpb - ROW0 - seq, d), x.dtype)], axis=1).reshape(batch * rpb, d)
    zeros_d = jnp.zeros((d,), F32)

    for i in range(DEPTH):
        m, j = i % N_MIXERS, i // N_MIXERS
        h = _ffn(h, norm_w[i, 0], ffn_w_gate, ffn_w_up, ffn_w_down, i, 0)
        if m == 0:
            p_main, gates = _norm_proj(h, norm_w[i, 1], mlstm_w_in[j], jnp.zeros((MLSTM_MAIN_W,), F32),
                                       MLSTM_MAIN_W, (mlstm_w_in[j][:, MLSTM_MAIN_W:], mlstm_b_if[j]))
            a = _mlstm_core(p_main, gates, mlstm_norm_w[j], batch, rpb)
            h = _proj_residual(a, mlstm_w_out[j], zeros_d, h, rpb)
        elif m == 1:
            h = _pool_mixer(h, norm_w[i, 1], pool_w[j], pool_scale[j], rpb)
        elif m == 2:
            p_main, side = _norm_proj(h, norm_w[i, 1], gdn_w_in[j], jnp.zeros((GDN_MAIN_W,), F32), GDN_MAIN_W,
                                      (gdn_w_in[j][:, GDN_MAIN_W:], jnp.zeros((2 * GDN_V_HEADS,), F32)))
            a = _gdn_core(p_main, side, gdn_conv_w[j], gdn_a_log[j], gdn_dt_bias[j], gdn_norm_w[j], batch, rpb)
            h = _proj_residual(a, gdn_w_out[j], zeros_d, h, rpb)
        else:
            (p_main,) = _norm_proj(h, norm_w[i, 1], swa_w_qkv[j], swa_b_qkv[j], SWA_IN)
            cos, sin = _rope_tables(rpb)
            a = _swa_core(p_main, swa_sinks[j], cos, sin, batch, rpb)
            h = _proj_residual(a, swa_w_out[j], swa_b_out[j], h, rpb)
        last = i == DEPTH - 1
        h = _ffn(h, norm_w[i, 2], ffn_w_gate, ffn_w_up, ffn_w_down, i, 1, final_norm_w if last else None)
    return h.reshape(batch, rpb, d)[:, ROW0:ROW0 + seq]
```

```python
import functools
import math

import jax
import jax.numpy as jnp
from jax import lax
from jax.experimental import pallas as pl
from jax.experimental.pallas import tpu as pltpu

F32 = jnp.float32
BF16 = jnp.bfloat16

D_MODEL = 2048
DEPTH = 4
N_MIXERS = 4
N_META = 16
EPS = 1e-6

MLSTM_HEADS = 8
MLSTM_DV = 256
MLSTM_DK = 128
MLSTM_QK_W = MLSTM_HEADS * MLSTM_DK
MLSTM_V_W = MLSTM_HEADS * MLSTM_DV
MLSTM_MAIN_W = 2 * MLSTM_QK_W + 2 * MLSTM_V_W

POOL_WINDOWS = (2, 4, 8, 16)
POOL_GROUP = D_MODEL // len(POOL_WINDOWS)
POOL_HALO = 16

GDN_DK = 128
GDN_DV = 128
GDN_QK_HEADS = 16
GDN_V_HEADS = 32
GDN_CONV = 4
GDN_QK_W = GDN_QK_HEADS * GDN_DK
GDN_V_W = GDN_V_HEADS * GDN_DV
GDN_CONV_CH = 2 * GDN_QK_W + GDN_V_W
GDN_MAIN_W = GDN_CONV_CH + GDN_V_W

SWA_DH = 64
SWA_HQ = 32
SWA_GROUP = 8
SWA_HKV = 4
SWA_WINDOW = 128
SWA_Q_W = SWA_HQ * SWA_DH
SWA_KV_W = SWA_HKV * SWA_DH
SWA_IN = SWA_Q_W + 2 * SWA_KV_W
ROPE_THETA = 10000.0

LEAD_PAD = 48
ROW0 = LEAD_PAD + N_META
SEQ_BLOCK = 128
LANES = 128

VMEM_LIMIT_BYTES = 56 * 1024 * 1024


def _rows_per_batch(seq):
    return -(-(ROW0 + seq) // SEQ_BLOCK) * SEQ_BLOCK


def _divisor_tile(total, target, multiple):
    best = None
    for t in range(multiple, min(total, target) + 1, multiple):
        if total % t == 0:
            best = t
    assert best is not None, (total, target, multiple)
    return best


def _params(*semantics):
    return pltpu.CompilerParams(dimension_semantics=semantics, vmem_limit_bytes=VMEM_LIMIT_BYTES)


def _rms_normalize(x):
    return x * lax.rsqrt(jnp.mean(x * x, axis=-1, keepdims=True) + EPS)


def _sigmoid(x):
    return 1.0 / (1.0 + jnp.exp(-x))


def _softplus(x):
    return jnp.maximum(x, 0.0) + jnp.log(1.0 + jnp.exp(-jnp.abs(x)))


def _dot(a, b):
    return jnp.dot(a, b, preferred_element_type=F32)


def _dot_nt(a, b):
    return lax.dot_general(a, b, (((1,), (1,)), ((), ())), preferred_element_type=F32)


def _dot_tn(a, b):
    return lax.dot_general(a, b, (((0,), (0,)), ((), ())), preferred_element_type=F32)


def _cumsum_rows(x):
    n = x.shape[0]
    row = lax.broadcasted_iota(jnp.int32, x.shape, 0)
    shift = 1
    while shift < n:
        x = x + jnp.where(row >= shift, pltpu.roll(x, shift, 0), 0.0)
        shift *= 2
    return x


def _column_to_row(col, eye):
    return jnp.sum(jnp.where(eye, col, 0.0), axis=0, keepdims=True)


def _ffn_kernel(h_ref, nw_ref, wg_ref, wu_ref, wd_ref, fw_ref, o_ref, xn_ref, *, final_norm):
    j = pl.program_id(1)

    @pl.when(j == 0)
    def _():
        x = h_ref[...]
        xn_ref[...] = (_rms_normalize(x) * nw_ref[...]).astype(BF16)
        o_ref[...] = x

    xn = xn_ref[...]
    g = _dot(xn, wg_ref[...])
    u = _dot(xn, wu_ref[...])
    a = (0.5 * g * _sigmoid(g) * u).astype(BF16)
    o_ref[...] += _dot(a, wd_ref[...])

    if final_norm:
        @pl.when(j == pl.num_programs(1) - 1)
        def _():
            o_ref[...] = _rms_normalize(o_ref[...]) * fw_ref[...]


def _ffn(h, nw, w_gate, w_up, w_down, layer, half, final_w=None, *, tm_target=1056, tf=256):
    rows, d = h.shape
    d_ff = w_gate.shape[-1]
    tm = _divisor_tile(rows, tm_target, 16)
    assert d_ff % tf == 0
    final_norm = final_w is not None
    fw = final_w if final_norm else nw
    kern = functools.partial(_ffn_kernel, final_norm=final_norm)
    return pl.pallas_call(
        kern,
        grid=(rows // tm, d_ff // tf),
        in_specs=[
            pl.BlockSpec((tm, d), lambda i, j: (i, 0)),
            pl.BlockSpec((1, d), lambda i, j: (0, 0)),
            pl.BlockSpec((None, None, d, tf), lambda i, j: (layer, half, 0, j)),
            pl.BlockSpec((None, None, d, tf), lambda i, j: (layer, half, 0, j)),
            pl.BlockSpec((None, None, tf, d), lambda i, j: (layer, half, j, 0)),
            pl.BlockSpec((1, d), lambda i, j: (0, 0)),
        ],
        out_specs=pl.BlockSpec((tm, d), lambda i, j: (i, 0)),
        out_shape=jax.ShapeDtypeStruct((rows, d), F32),
        scratch_shapes=[pltpu.VMEM((tm, d), BF16)],
        compiler_params=_params("parallel", "arbitrary"),
        name="ffn",
    )(h, nw.reshape(1, d), w_gate, w_up, w_down, fw.reshape(1, d))


def _norm_proj_kernel(*refs, has_side):
    if has_side:
        h_ref, nw_ref, w_ref, b_ref, ws_ref, bs_ref, o_ref, os_ref, xn_ref = refs
    else:
        h_ref, nw_ref, w_ref, b_ref, o_ref, xn_ref = refs
    j = pl.program_id(1)

    @pl.when(j == 0)
    def _():
        xn = (_rms_normalize(h_ref[...]) * nw_ref[...]).astype(BF16)
        xn_ref[...] = xn
        if has_side:
            os_ref[...] = _dot(xn, ws_ref[...]) + bs_ref[...]

    o_ref[...] = (_dot(xn_ref[...], w_ref[...]) + b_ref[...]).astype(o_ref.dtype)


def _norm_proj(h, nw, w, bias, n_main, side=None, *, tm_target=1056, tn_target=1024):
    rows, d = h.shape
    tm = _divisor_tile(rows, tm_target, 16)
    tn = _divisor_tile(n_main, tn_target, 2 * LANES)
    in_specs = [
        pl.BlockSpec((tm, d), lambda i, j: (i, 0)),
        pl.BlockSpec((1, d), lambda i, j: (0, 0)),
        pl.BlockSpec((d, tn), lambda i, j: (0, j)),
        pl.BlockSpec((1, tn), lambda i, j: (0, j)),
    ]
    out_specs = [pl.BlockSpec((tm, tn), lambda i, j: (i, j))]
    out_shape = [jax.ShapeDtypeStruct((rows, n_main), BF16)]
    args = [h, nw.reshape(1, d), w, bias.reshape(1, -1)]
    if side is not None:
        w_side, b_side = side
        n_side = w_side.shape[1]
        in_specs += [pl.BlockSpec((d, n_side), lambda i, j: (0, 0)), pl.BlockSpec((1, n_side), lambda i, j: (0, 0))]
        out_specs += [pl.BlockSpec((tm, n_side), lambda i, j: (i, 0))]
        out_shape += [jax.ShapeDtypeStruct((rows, n_side), F32)]
        args += [w_side, b_side.reshape(1, n_side)]
    return pl.pallas_call(
        functools.partial(_norm_proj_kernel, has_side=side is not None),
        grid=(rows // tm, n_main // tn),
        in_specs=in_specs,
        out_specs=out_specs,
        out_shape=out_shape,
        scratch_shapes=[pltpu.VMEM((tm, d), BF16)],
        compiler_params=_params("parallel", "arbitrary"),
        name="norm_proj",
    )(*args)


def _proj_residual_kernel(a_ref, w_ref, b_ref, h_ref, o_ref, *, tm, rows_per_batch):
    i = pl.program_id(0)
    y = _dot(a_ref[...], w_ref[...]) + b_ref[...]
    row = lax.rem(i * tm, rows_per_batch) + lax.broadcasted_iota(jnp.int32, (tm, 1), 0)
    o_ref[...] = h_ref[...] + jnp.where(row >= LEAD_PAD, y, 0.0)


def _proj_residual(a, w, bias, h, rows_per_batch, *, tm_target=1056, tn=512):
    rows, k = a.shape
    d = h.shape[1]
    tm = _divisor_tile(rows_per_batch, tm_target, 16)
    kern = functools.partial(_proj_residual_kernel, tm=tm, rows_per_batch=rows_per_batch)
    return pl.pallas_call(
        kern,
        grid=(rows // tm, d // tn),
        in_specs=[
            pl.BlockSpec((tm, k), lambda i, j: (i, 0)),
            pl.BlockSpec((k, tn), lambda i, j: (0, j)),
            pl.BlockSpec((1, tn), lambda i, j: (0, j)),
            pl.BlockSpec((tm, tn), lambda i, j: (i, j)),
        ],
        out_specs=pl.BlockSpec((tm, tn), lambda i, j: (i, j)),
        out_shape=jax.ShapeDtypeStruct((rows, d), F32),
        compiler_params=_params("parallel", "arbitrary"),
        name="proj_residual",
    )(a, w, bias.reshape(1, d), h)


def _mlstm_kernel(q_ref, k_ref, v_ref, og_ref, g_ref, nw_ref, o_ref, c_ref, n_ref, m_ref, *, blk):
    c = pl.program_id(1)

    @pl.when(c == 0)
    def _():
        c_ref[...] = jnp.zeros_like(c_ref)
        n_ref[...] = jnp.zeros_like(n_ref)
        m_ref[...] = jnp.zeros_like(m_ref)

    gates = g_ref[...]
    row = c * blk + lax.broadcasted_iota(jnp.int32, (blk, 1), 0)
    valid = row >= LEAD_PAD
    log_f = jnp.where(valid, jnp.minimum(gates, 0.0) - jnp.log(1.0 + jnp.exp(-jnp.abs(gates))), 0.0)
    b_all = _cumsum_rows(log_f)
    li_all = jnp.where(valid, gates, -jnp.inf)

    ri = lax.broadcasted_iota(jnp.int32, (blk, blk), 0)
    ci = lax.broadcasted_iota(jnp.int32, (blk, blk), 1)
    eye = ri == ci
    causal = ci <= ri
    scale = MLSTM_DK ** -0.5

    for hd in range(MLSTM_HEADS):
        li_c = li_all[:, hd:hd + 1]
        b_c = b_all[:, MLSTM_HEADS + hd:MLSTM_HEADS + hd + 1]
        d_c = li_c - b_c
        d_r = _column_to_row(d_c, eye)
        m_prev = m_ref[hd][:, :1]
        log_w = jnp.where(causal, b_c + d_r, -jnp.inf)
        log_init = b_c + m_prev
        m_t = jnp.maximum(log_init, jnp.max(log_w, axis=-1, keepdims=True))
        w = jnp.exp(log_w - m_t) * scale
        w_init = jnp.exp(log_init - m_t) * scale

        qh = q_ref[:, hd * MLSTM_DK:(hd + 1) * MLSTM_DK]
        kh = k_ref[:, hd * MLSTM_DK:(hd + 1) * MLSTM_DK]
        vh = v_ref[:, hd * MLSTM_DV:(hd + 1) * MLSTM_DV]
        c_st = c_ref[hd]
        n_st = n_ref[hd]
        qk = _dot_nt(qh, kh) * w
        num = w_init * _dot(qh, c_st.astype(BF16)) + _dot(qk.astype(BF16), vh)
        qn = jnp.sum(qh.astype(F32) * n_st, axis=-1, keepdims=True)
        den = w_init * qn + jnp.sum(qk, axis=-1, keepdims=True)
        hh = num / jnp.maximum(jnp.abs(den), jnp.exp(-m_t))

        b_last = b_c[blk - 1:blk, :]
        log_end_init = b_last + m_prev
        log_end = b_last + d_c
        m_new = jnp.maximum(log_end_init, jnp.max(log_end, axis=0, keepdims=True))
        a_init = jnp.exp(log_end_init - m_new)
        ka = kh.astype(F32) * jnp.exp(log_end - m_new)
        c_ref[hd] = a_init * c_st + _dot_tn(ka.astype(BF16), vh)
        n_ref[hd] = a_init * n_st + jnp.sum(ka, axis=0, keepdims=True)
        m_ref[hd] = jnp.broadcast_to(m_new, (1, LANES))

        sl = slice(hd * MLSTM_DV, (hd + 1) * MLSTM_DV)
        gate = _sigmoid(og_ref[:, sl].astype(F32))
        o_ref[:, sl] = (_rms_normalize(hh) * nw_ref[:, sl] * gate).astype(o_ref.dtype)


def _mlstm_core(p_main, gates, norm_w, batch, rows_per_batch):
    blk = SEQ_BLOCK
    nblk = rows_per_batch // blk
    rows = p_main.shape[0]
    idx = lambda col: (lambda b, c: (b * nblk + c, col))
    kern = functools.partial(_mlstm_kernel, blk=blk)
    return pl.pallas_call(
        kern,
        grid=(batch, nblk),
        in_specs=[
            pl.BlockSpec((blk, MLSTM_QK_W), idx(0)),
            pl.BlockSpec((blk, MLSTM_QK_W), idx(1)),
            pl.BlockSpec((blk, MLSTM_V_W), idx(1)),
            pl.BlockSpec((blk, MLSTM_V_W), idx(2)),
            pl.BlockSpec((blk, 2 * MLSTM_HEADS), idx(0)),
            pl.BlockSpec((1, MLSTM_V_W), lambda b, c: (0, 0)),
        ],
        out_specs=pl.BlockSpec((blk, MLSTM_V_W), idx(0)),
        out_shape=jax.ShapeDtypeStruct((rows, MLSTM_V_W), BF16),
        scratch_shapes=[
            pltpu.VMEM((MLSTM_HEADS, MLSTM_DK, MLSTM_DV), F32),
            pltpu.VMEM((MLSTM_HEADS, 1, MLSTM_DK), F32),
            pltpu.VMEM((MLSTM_HEADS, 1, LANES), F32),
        ],
        compiler_params=_params("parallel", "arbitrary"),
        name="mlstm_core",
    )(p_main, p_main, p_main, p_main, gates, norm_w.reshape(1, MLSTM_V_W))


def _pool_kernel(h_ref, halo_ref, nw_ref, w_ref, sc_ref, o_ref, ext_ref, *, tm, rows_per_batch):
    i = pl.program_id(0)
    x = h_ref[...]
    nw = nw_ref[...]
    u = _rms_normalize(x) * nw
    ext_ref[0:POOL_HALO, :] = _rms_normalize(halo_ref[...]) * nw
    ext_ref[POOL_HALO:, :] = u
    pos = lax.rem(i * tm, rows_per_batch) + lax.broadcasted_iota(jnp.int32, (tm, 1), 0) - LEAD_PAD
    valid = pos >= 0
    for gi, win in enumerate(POOL_WINDOWS):
        sl = slice(gi * POOL_GROUP, (gi + 1) * POOL_GROUP)
        terms = [ext_ref[POOL_HALO - s:POOL_HALO - s + tm, sl] for s in range(win)]
        while len(terms) > 1:
            terms = [terms[a] + terms[a + 1] for a in range(0, len(terms), 2)]
        count = jnp.clip(pos + 1, 1, win).astype(F32)
        pooled = terms[0] / count - u[:, sl]
        y = _dot(pooled.astype(BF16), w_ref[gi]) * sc_ref[:, sl]
        o_ref[:, sl] = x[:, sl] + jnp.where(valid, y, 0.0)


def _pool_mixer(h, nw, w_group, scale, rows_per_batch, *, tm_target=384):
    rows, d = h.shape
    tm = _divisor_tile(rows_per_batch, tm_target, POOL_HALO)
    kern = functools.partial(_pool_kernel, tm=tm, rows_per_batch=rows_per_batch)
    halo_blocks = tm // POOL_HALO
    return pl.pallas_call(
        kern,
        grid=(rows // tm,),
        in_specs=[
            pl.BlockSpec((tm, d), lambda i: (i, 0)),
            pl.BlockSpec((POOL_HALO, d), lambda i: (jnp.maximum(i * halo_blocks - 1, 0), 0)),
            pl.BlockSpec((1, d), lambda i: (0, 0)),
            pl.BlockSpec(w_group.shape, lambda i: (0, 0, 0)),
            pl.BlockSpec((1, d), lambda i: (0, 0)),
        ],
        out_specs=pl.BlockSpec((tm, d), lambda i: (i, 0)),
        out_shape=jax.ShapeDtypeStruct((rows, d), F32),
        scratch_shapes=[pltpu.VMEM((tm + POOL_HALO, d), F32)],
        compiler_params=_params("parallel"),
        name="pool_mixer",
    )(h, h, nw.reshape(1, d), w_group, scale.reshape(1, d))


GDN_CARRY = 8


def _unit_lower_inverses(strict_lowers):
    n = strict_lowers[0].shape[0]
    ri = lax.broadcasted_iota(jnp.int32, (n, n), 0)
    ci = lax.broadcasted_iota(jnp.int32, (n, n), 1)
    ident = jnp.where(ri == ci, 1.0, 0.0)
    powers = [-l for l in strict_lowers]
    invs = [ident + a for a in powers]
    terms = 2
    while terms < n:
        powers_b = [a.astype(BF16) for a in powers]
        powers = [_dot(ab, ab) for ab in powers_b]
        invs = [p + _dot(p.astype(BF16), a.astype(BF16)) for p, a in zip(invs, powers)]
        terms *= 2
    return invs


def _gdn_kernel(x_ref, z_ref, e_ref, cw_ref, al_ref, dt_ref, nw_ref, o_ref,
                ext_ref, s_ref, q_s, k_s, v_s, o_s, *, blk, head_group):
    c = pl.program_id(1)

    @pl.when(c == 0)
    def _():
        ext_ref[0:GDN_CARRY, :] = jnp.zeros((GDN_CARRY, GDN_CONV_CH), F32)
        s_ref[...] = jnp.zeros_like(s_ref)

    ext_ref[GDN_CARRY:, :] = x_ref[...].astype(F32)
    n_cols = GDN_CONV_CH // LANES
    for cb in range(n_cols):
        sl = slice(cb * LANES, (cb + 1) * LANES)
        acc = None
        for tap in range(GDN_CONV):
            off = GDN_CARRY - (GDN_CONV - 1) + tap
            term = ext_ref[off:off + blk, sl] * cw_ref[tap:tap + 1, sl]
            acc = term if acc is None else acc + term
        y = acc * _sigmoid(acc)
        if cb < 2 * GDN_QK_HEADS:
            y = y * lax.rsqrt(jnp.sum(y * y, axis=-1, keepdims=True) + EPS)
            if cb < GDN_QK_HEADS:
                q_s[cb] = (y * (GDN_DK ** -0.5)).astype(BF16)
            else:
                k_s[cb - GDN_QK_HEADS] = y.astype(BF16)
        else:
            v_s[cb - 2 * GDN_QK_HEADS] = y.astype(BF16)
    ext_ref[0:GDN_CARRY, :] = ext_ref[blk:blk + GDN_CARRY, :]

    row = c * blk + lax.broadcasted_iota(jnp.int32, (blk, 1), 0)
    valid = row >= LEAD_PAD
    e = e_ref[...]
    beta_all = jnp.where(valid, _sigmoid(e[:, :GDN_V_HEADS]), 0.0)
    g_all = jnp.where(valid, -jnp.exp(al_ref[...]) * _softplus(e[:, GDN_V_HEADS:] + dt_ref[...]), 0.0)
    gc_all = _cumsum_rows(g_all)

    ri = lax.broadcasted_iota(jnp.int32, (blk, blk), 0)
    ci = lax.broadcasted_iota(jnp.int32, (blk, blk), 1)
    eye = ri == ci
    causal = ci <= ri
    strict = ci < ri
    rep = GDN_V_HEADS // GDN_QK_HEADS
    for g0 in range(0, GDN_V_HEADS, head_group):
        vheads = range(g0, g0 + head_group)
        qheads = range(g0 // rep, (g0 + head_group) // rep)
        q = {j: q_s[j] for j in qheads}
        k = {j: k_s[j] for j in qheads}
        kk = {j: _dot_nt(k[j], k[j]) for j in qheads}
        qk = {j: _dot_nt(q[j], k[j]) for j in qheads}
        gc_c = {jv: gc_all[:, jv:jv + 1] for jv in vheads}
        beta_c = {jv: beta_all[:, jv:jv + 1] for jv in vheads}
        decay = {jv: jnp.exp(jnp.where(causal, gc_c[jv] - _column_to_row(gc_c[jv], eye), -jnp.inf)) for jv in vheads}
        t_inv = _unit_lower_inverses([jnp.where(strict, kk[jv // rep] * decay[jv], 0.0) * beta_c[jv] for jv in vheads])
        t_inv = dict(zip(vheads, t_inv))
        egc = {jv: jnp.exp(gc_c[jv]) for jv in vheads}
        s_st = {jv: s_ref[jv] for jv in vheads}
        s_b = {jv: s_st[jv].astype(BF16) for jv in vheads}
        qs = {jv: _dot(q[jv // rep], s_b[jv]) * egc[jv] for jv in vheads}
        rhs = {jv: jnp.concatenate([v_s[jv].astype(F32) * beta_c[jv],
                                    k[jv // rep].astype(F32) * (beta_c[jv] * egc[jv])], axis=-1).astype(BF16)
               for jv in vheads}
        sol = {jv: _dot(t_inv[jv].astype(BF16), rhs[jv]) for jv in vheads}
        v_new = {jv: sol[jv][:, :GDN_DV] - _dot(sol[jv][:, GDN_DV:].astype(BF16), s_b[jv]) for jv in vheads}
        for jv in vheads:
            o_s[jv] = qs[jv] + _dot((qk[jv // rep] * decay[jv]).astype(BF16), v_new[jv].astype(BF16))
        for jv in vheads:
            g_last = gc_c[jv][blk - 1:blk, :]
            v_dec = (v_new[jv] * jnp.exp(g_last - gc_c[jv])).astype(BF16)
            s_ref[jv] = jnp.exp(g_last) * s_st[jv] + _dot_tn(k[jv // rep], v_dec)

    for jv in range(GDN_V_HEADS):
        sl = slice(jv * GDN_DV, (jv + 1) * GDN_DV)
        z = z_ref[:, sl].astype(F32)
        o_ref[:, sl] = (_rms_normalize(o_s[jv]) * nw_ref[...] * (z * _sigmoid(z))).astype(o_ref.dtype)


def _gdn_core(p_main, side, conv_w, a_log, dt_bias, norm_w, batch, rows_per_batch, *, blk=64, head_group=16):
    nblk = rows_per_batch // blk
    rows = p_main.shape[0]
    idx = lambda col: (lambda b, c: (b * nblk + c, col))
    const2 = lambda b, c: (0, 0)
    kern = functools.partial(_gdn_kernel, blk=blk, head_group=head_group)
    return pl.pallas_call(
        kern,
        grid=(batch, nblk),
        in_specs=[
            pl.BlockSpec((blk, GDN_CONV_CH), idx(0)),
            pl.BlockSpec((blk, GDN_V_W), idx(GDN_CONV_CH // GDN_V_W)),
            pl.BlockSpec((blk, 2 * GDN_V_HEADS), idx(0)),
            pl.BlockSpec((GDN_CONV, GDN_CONV_CH), const2),
            pl.BlockSpec((1, GDN_V_HEADS), const2),
            pl.BlockSpec((1, GDN_V_HEADS), const2),
            pl.BlockSpec((1, GDN_DV), const2),
        ],
        out_specs=pl.BlockSpec((blk, GDN_V_W), idx(0)),
        out_shape=jax.ShapeDtypeStruct((rows, GDN_V_W), BF16),
        scratch_shapes=[
            pltpu.VMEM((blk + GDN_CARRY, GDN_CONV_CH), F32),
            pltpu.VMEM((GDN_V_HEADS, GDN_DK, GDN_DV), F32),
            pltpu.VMEM((GDN_QK_HEADS, blk, GDN_DK), BF16),
            pltpu.VMEM((GDN_QK_HEADS, blk, GDN_DK), BF16),
            pltpu.VMEM((GDN_V_HEADS, blk, GDN_DV), BF16),
            pltpu.VMEM((GDN_V_HEADS, blk, GDN_DV), F32),
        ],
        compiler_params=_params("parallel", "arbitrary"),
        name="gdn_core",
    )(p_main, p_main, side, conv_w, a_log.reshape(1, -1), dt_bias.reshape(1, -1), norm_w.reshape(1, -1))


def _rope_table_kernel(cos_ref, sin_ref):
    shape = cos_ref.shape
    pos = (lax.broadcasted_iota(jnp.int32, shape, 0) - LEAD_PAD).astype(F32)
    pair = lax.rem(lax.broadcasted_iota(jnp.int32, shape, 1), SWA_DH // 2).astype(F32)
    inv = jnp.exp(pair * (-2.0 * math.log(ROPE_THETA) / SWA_DH))
    ang = pos * inv
    cos_ref[...] = jnp.cos(ang)
    sin_ref[...] = jnp.sin(ang)


def _rope_tables(rows_per_batch):
    shape = jax.ShapeDtypeStruct((rows_per_batch, LANES), F32)
    return pl.pallas_call(_rope_table_kernel, out_shape=[shape, shape], name="rope_tables")()


def _swa_kernel(sink_ref, q_ref, kv_ref, kvp_ref, cos_ref, sin_ref, cosp_ref, sinp_ref, o_ref):
    n = pl.program_id(1)
    blk = SEQ_BLOCK
    lane = lax.broadcasted_iota(jnp.int32, (1, LANES), 1)
    low = lane < SWA_DH
    first_half = lax.rem(lane, SWA_DH) < SWA_DH // 2

    def rope(x, cos, sin):
        rot = jnp.where(first_half, -pltpu.roll(x, LANES - SWA_DH // 2, 1), pltpu.roll(x, SWA_DH // 2, 1))
        return x * cos + rot * sin

    cos, sin = cos_ref[...], sin_ref[...]
    cosp, sinp = cosp_ref[...], sinp_ref[...]

    stacked = lax.broadcasted_iota(jnp.int32, (2 * blk, 2 * blk), 0)
    q_row = n * blk + jnp.where(stacked < blk, stacked, stacked - blk)
    k_row = (n - 1) * blk + lax.broadcasted_iota(jnp.int32, (2 * blk, 2 * blk), 1)
    mask = (k_row <= q_row) & (q_row - k_row < SWA_WINDOW) & (k_row >= LEAD_PAD)
    top = lax.broadcasted_iota(jnp.int32, (2 * blk, 1), 0) < blk

    def both_heads(x, head_in_pair):
        swapped = pltpu.roll(x, SWA_DH, 1)
        return jnp.where(low, x, swapped) if head_in_pair == 0 else jnp.where(low, swapped, x)

    for g in range(SWA_HKV):
        pair_col, head_in_pair = divmod(g, 2)
        ksl = slice(pair_col * LANES, (pair_col + 1) * LANES)
        vsl = slice(SWA_KV_W + pair_col * LANES, SWA_KV_W + (pair_col + 1) * LANES)
        k_cur = both_heads(rope(kv_ref[:, ksl].astype(F32), cos, sin), head_in_pair)
        k_prev = both_heads(rope(kvp_ref[:, ksl].astype(F32), cosp, sinp), head_in_pair)
        kk = jnp.concatenate([k_prev, k_cur], axis=0).astype(BF16)
        v_cur = both_heads(kv_ref[:, vsl].astype(F32), head_in_pair)
        v_prev = both_heads(kvp_ref[:, vsl].astype(F32), head_in_pair)
        v2 = jnp.concatenate([v_prev, v_cur], axis=0)
        v_stack = jnp.concatenate([jnp.where(low, v2, 0.0), jnp.where(low, 0.0, v2)], axis=0).astype(BF16)
        for p in range(SWA_GROUP // 2):
            col = g * (SWA_GROUP // 2) + p
            qsl = slice(col * LANES, (col + 1) * LANES)
            qp = rope(q_ref[:, qsl].astype(F32), cos, sin)
            q2 = jnp.concatenate([jnp.where(low, qp, 0.0), jnp.where(low, 0.0, qp)], axis=0).astype(BF16)
            s = _dot_nt(q2, kk) * (SWA_DH ** -0.5)
            s = jnp.where(mask, s, -jnp.inf)
            sink = jnp.where(top, sink_ref[2 * col], sink_ref[2 * col + 1])
            m = jnp.maximum(jnp.max(s, axis=-1, keepdims=True), sink)
            e = jnp.exp(s - m)
            prob = e / (jnp.sum(e, axis=-1, keepdims=True) + jnp.exp(sink - m))
            p2 = jnp.concatenate([prob[:blk], prob[blk:]], axis=1).astype(BF16)
            o_ref[:, qsl] = _dot(p2, v_stack).astype(o_ref.dtype)


def _swa_core(p_main, sinks, cos, sin, batch, rows_per_batch):
    blk = SEQ_BLOCK
    nblk = rows_per_batch // blk
    rows = p_main.shape[0]
    kv_col = SWA_Q_W // (2 * SWA_KV_W)
    cur = lambda col: (lambda b, n: (b * nblk + n, col))
    prev = lambda col: (lambda b, n: (jnp.maximum(b * nblk + n - 1, 0), col))
    tab_cur = lambda b, n: (n, 0)
    tab_prev = lambda b, n: (jnp.maximum(n - 1, 0), 0)
    return pl.pallas_call(
        _swa_kernel,
        grid=(batch, nblk),
        in_specs=[
            pl.BlockSpec(memory_space=pltpu.SMEM),
            pl.BlockSpec((blk, SWA_Q_W), cur(0)),
            pl.BlockSpec((blk, 2 * SWA_KV_W), cur(kv_col)),
            pl.BlockSpec((blk, 2 * SWA_KV_W), prev(kv_col)),
            pl.BlockSpec((blk, LANES), tab_cur),
            pl.BlockSpec((blk, LANES), tab_cur),
            pl.BlockSpec((blk, LANES), tab_prev),
            pl.BlockSpec((blk, LANES), tab_prev),
        ],
        out_specs=pl.BlockSpec((blk, SWA_Q_W), cur(0)),
        out_shape=jax.ShapeDtypeStruct((rows, SWA_Q_W), BF16),
        compiler_params=_params("parallel", "parallel"),
        name="swa_core",
    )(sinks, p_main, p_main, p_main, cos, sin, cos, sin)


def kernel(x, meta_tokens, norm_w, ffn_w_gate, ffn_w_up, ffn_w_down, mlstm_w_in, mlstm_b_if, mlstm_norm_w, mlstm_w_out, pool_w, pool_scale, gdn_w_in, gdn_conv_w, gdn_a_log, gdn_dt_bias, gdn_norm_w, gdn_w_out, swa_w_qkv, swa_b_qkv, swa_sinks, swa_w_out, swa_b_out, final_norm_w):
    batch, seq, d = x.shape
    rpb = _rows_per_batch(seq)
    meta = jnp.broadcast_to(meta_tokens.astype(x.dtype)[None], (batch, N_META, d))
    h = jnp.concatenate([jnp.zeros((batch, LEAD_PAD, d), x.dtype), meta, x,
                         jnp.zeros((batch, rpb - ROW0 - seq, d), x.dtype)], axis=1).reshape(batch * rpb, d)
    zeros_d = jnp.zeros((d,), F32)
    (ffn_w_gate, ffn_w_up, ffn_w_down, mlstm_w_in, mlstm_w_out, pool_w, gdn_w_in, gdn_w_out, swa_w_qkv,
     swa_w_out) = (w.astype(BF16) for w in (ffn_w_gate, ffn_w_up, ffn_w_down, mlstm_w_in, mlstm_w_out, pool_w,
                                            gdn_w_in, gdn_w_out, swa_w_qkv, swa_w_out))

    for i in range(DEPTH):
        m, j = i % N_MIXERS, i // N_MIXERS
        h = _ffn(h, norm_w[i, 0], ffn_w_gate, ffn_w_up, ffn_w_down, i, 0)
        if m == 0:
            p_main, gates = _norm_proj(h, norm_w[i, 1], mlstm_w_in[j], jnp.zeros((MLSTM_MAIN_W,), F32),
                                       MLSTM_MAIN_W, (mlstm_w_in[j][:, MLSTM_MAIN_W:], mlstm_b_if[j]))
            a = _mlstm_core(p_main, gates, mlstm_norm_w[j], batch, rpb)
            h = _proj_residual(a, mlstm_w_out[j], zeros_d, h, rpb)
        elif m == 1:
            h = _pool_mixer(h, norm_w[i, 1], pool_w[j], pool_scale[j], rpb)
        elif m == 2:
            p_main, side = _norm_proj(h, norm_w[i, 1], gdn_w_in[j], jnp.zeros((GDN_MAIN_W,), F32), GDN_MAIN_W,
                                      (gdn_w_in[j][:, GDN_MAIN_W:], jnp.zeros((2 * GDN_V_HEADS,), F32)))
            a = _gdn_core(p_main, side, gdn_conv_w[j], gdn_a_log[j], gdn_dt_bias[j], gdn_norm_w[j], batch, rpb)
            h = _proj_residual(a, gdn_w_out[j], zeros_d, h, rpb)
        else:
            (p_main,) = _norm_proj(h, norm_w[i, 1], swa_w_qkv[j], swa_b_qkv[j], SWA_IN)
            cos, sin = _rope_tables(rpb)
            a = _swa_core(p_main, swa_sinks[j], cos, sin, batch, rpb)
            h = _proj_residual(a, swa_w_out[j], swa_b_out[j], h, rpb)
        last = i == DEPTH - 1
        h = _ffn(h, norm_w[i, 2], ffn_w_gate, ffn_w_up, ffn_w_down, i, 1, final_norm_w if last else None)
    return h.reshape(batch, rpb, d)[:, ROW0:ROW0 + seq]
```

```python
import functools
import math

import jax
import jax.numpy as jnp
from jax import lax
from jax.experimental import pallas as pl
from jax.experimental.pallas import tpu as pltpu

F32 = jnp.float32
BF16 = jnp.bfloat16

D_MODEL = 2048
DEPTH = 4
N_MIXERS = 4
N_META = 16
EPS = 1e-6

MLSTM_HEADS = 8
MLSTM_DV = 256
MLSTM_DK = 128
MLSTM_QK_W = MLSTM_HEADS * MLSTM_DK
MLSTM_V_W = MLSTM_HEADS * MLSTM_DV
MLSTM_MAIN_W = 2 * MLSTM_QK_W + 2 * MLSTM_V_W

POOL_WINDOWS = (2, 4, 8, 16)
POOL_GROUP = D_MODEL // len(POOL_WINDOWS)
POOL_HALO = 16

GDN_DK = 128
GDN_DV = 128
GDN_QK_HEADS = 16
GDN_V_HEADS = 32
GDN_CONV = 4
GDN_CHUNK = 64
GDN_QK_W = GDN_QK_HEADS * GDN_DK
GDN_V_W = GDN_V_HEADS * GDN_DV
GDN_CONV_CH = 2 * GDN_QK_W + GDN_V_W
GDN_MAIN_W = GDN_CONV_CH + GDN_V_W

SWA_DH = 64
SWA_HQ = 32
SWA_GROUP = 8
SWA_HKV = 4
SWA_WINDOW = 128
SWA_Q_W = SWA_HQ * SWA_DH
SWA_KV_W = SWA_HKV * SWA_DH
SWA_IN = SWA_Q_W + 2 * SWA_KV_W
ROPE_THETA = 10000.0

LANES = 128
SEQ_BLOCK = 128
LEAD_PAD = SEQ_BLOCK - N_META

VMEM_LIMIT_BYTES = 56 * 1024 * 1024
FFN_PROLOGUE_ROWS = 384


def _rows_per_batch(seq):
    assert seq % SEQ_BLOCK == 0, seq
    return seq + SEQ_BLOCK


def _divisor_tile(total, target, multiple):
    best = None
    for t in range(multiple, min(total, target) + 1, multiple):
        if total % t == 0:
            best = t
    assert best is not None, (total, target, multiple)
    return best


def _params(*semantics):
    return pltpu.CompilerParams(dimension_semantics=semantics, vmem_limit_bytes=VMEM_LIMIT_BYTES)


def _physical_block(b, n, nblk):
    return b * nblk + jnp.where(n == 0, nblk - 1, n - 1)


def _is_token_row(row_in_batch, seq):
    return (row_in_batch < seq) | (row_in_batch >= seq + LEAD_PAD)


def _rms_normalize(x):
    return x * lax.rsqrt(jnp.mean(x * x, axis=-1, keepdims=True) + EPS)


def _sigmoid(x):
    return 1.0 / (1.0 + jnp.exp(-x))


def _softplus(x):
    return jnp.maximum(x, 0.0) + jnp.log(1.0 + jnp.exp(-jnp.abs(x)))


def _dot(a, b):
    return jnp.dot(a, b, preferred_element_type=F32)


def _dot_nt(a, b):
    return lax.dot_general(a, b, (((1,), (1,)), ((), ())), preferred_element_type=F32)


def _dot_tn(a, b):
    return lax.dot_general(a, b, (((0,), (0,)), ((), ())), preferred_element_type=F32)


def _cumsum_rows(x):
    n = x.shape[0]
    row = lax.broadcasted_iota(jnp.int32, x.shape, 0)
    shift = 1
    while shift < n:
        x = x + jnp.where(row >= shift, pltpu.roll(x, shift, 0), 0.0)
        shift *= 2
    return x


def _column_to_row(col, eye):
    return jnp.sum(jnp.where(eye, col, 0.0), axis=0, keepdims=True)


def _ffn_kernel(*refs, assemble, final_norm, tiles_per_batch):
    h_ref, nw_ref, wg_ref, wu_ref, wd_ref = refs[:5]
    rest = list(refs[5:])
    meta_ref = rest.pop(0) if assemble else None
    fw_ref = rest.pop(0) if final_norm else None
    o_ref, xn_ref = rest
    j = pl.program_id(1)

    @pl.when(j == 0)
    def _():
        tm = o_ref.shape[0]
        sub = _divisor_tile(tm, FFN_PROLOGUE_ROWS, 16)
        if assemble:
            assert sub >= SEQ_BLOCK
            for r0, r1 in [(r, r + sub) for r in range(0, tm - sub, sub)] + [(tm - sub, tm - SEQ_BLOCK)]:
                if r1 > r0:
                    o_ref[r0:r1, :] = h_ref[r0:r1, :]
            ends_batch = lax.rem(pl.program_id(0), tiles_per_batch) == tiles_per_batch - 1

            @pl.when(ends_batch)
            def _():
                o_ref[tm - SEQ_BLOCK:tm - N_META, :] = jnp.zeros((LEAD_PAD, o_ref.shape[1]), F32)
                o_ref[tm - N_META:, :] = meta_ref[...]

            @pl.when(jnp.logical_not(ends_batch))
            def _():
                o_ref[tm - SEQ_BLOCK:, :] = h_ref[tm - SEQ_BLOCK:, :]
        else:
            for r in range(0, tm, sub):
                o_ref[r:r + sub, :] = h_ref[r:r + sub, :]
        for r in range(0, tm, sub):
            xn_ref[r:r + sub, :] = (_rms_normalize(o_ref[r:r + sub, :]) * nw_ref[...]).astype(BF16)

    xn = xn_ref[...]
    g = _dot(xn, wg_ref[...].astype(BF16))
    u = _dot(xn, wu_ref[...].astype(BF16))
    a = (0.5 * g * _sigmoid(g) * u).astype(BF16)
    o_ref[...] += _dot(a, wd_ref[...].astype(BF16))

    if final_norm:
        @pl.when(j == pl.num_programs(1) - 1)
        def _():
            o_ref[...] = _rms_normalize(o_ref[...]) * fw_ref[...]


def _ffn(h, nw, w_gate, w_up, w_down, layer, half, rows_per_batch, *, meta=None, final_w=None,
         tm_target=1056, tf=256):
    batch, _, d = h.shape
    d_ff = w_gate.shape[-1]
    tm = _divisor_tile(rows_per_batch, tm_target, SEQ_BLOCK // 8)
    assert tm >= SEQ_BLOCK and d_ff % tf == 0
    tpb = rows_per_batch // tm
    row_spec = pl.BlockSpec((None, tm, d), lambda i, j: (i // tpb, i % tpb, 0))
    vec_spec = pl.BlockSpec((1, d), lambda i, j: (0, 0))
    in_specs = [
        row_spec, vec_spec,
        pl.BlockSpec((None, None, d, tf), lambda i, j: (layer, half, 0, j)),
        pl.BlockSpec((None, None, d, tf), lambda i, j: (layer, half, 0, j)),
        pl.BlockSpec((None, None, tf, d), lambda i, j: (layer, half, j, 0)),
    ]
    args = [h, nw.reshape(1, d), w_gate, w_up, w_down]
    if meta is not None:
        in_specs.append(pl.BlockSpec((N_META, d), lambda i, j: (0, 0)))
        args.append(meta)
    if final_w is not None:
        in_specs.append(vec_spec)
        args.append(final_w.reshape(1, d))
    out_rows = rows_per_batch - SEQ_BLOCK if final_w is not None else rows_per_batch
    kern = functools.partial(_ffn_kernel, assemble=meta is not None, final_norm=final_w is not None,
                             tiles_per_batch=tpb)
    return pl.pallas_call(
        kern,
        grid=(batch * tpb, d_ff // tf),
        in_specs=in_specs,
        out_specs=row_spec,
        out_shape=jax.ShapeDtypeStruct((batch, out_rows, d), F32),
        scratch_shapes=[pltpu.VMEM((tm, d), BF16)],
        compiler_params=_params("parallel", "arbitrary"),
        name="ffn",
    )(*args)


def _norm_proj_kernel(*refs, has_side):
    if has_side:
        h_ref, nw_ref, w_ref, b_ref, ws_ref, bs_ref, o_ref, os_ref, xn_ref = refs
    else:
        h_ref, nw_ref, w_ref, b_ref, o_ref, xn_ref = refs
    j = pl.program_id(1)

    @pl.when(j == 0)
    def _():
        xn = (_rms_normalize(h_ref[...]) * nw_ref[...]).astype(BF16)
        xn_ref[...] = xn
        if has_side:
            os_ref[...] = _dot(xn, ws_ref[...]) + bs_ref[...]

    o_ref[...] = (_dot(xn_ref[...], w_ref[...]) + b_ref[...]).astype(o_ref.dtype)


def _norm_proj(h, nw, w, bias, n_main, side=None, *, tm_target=1056, tn_target=1024):
    rows, d = h.shape
    tm = _divisor_tile(rows, tm_target, 16)
    tn = _divisor_tile(n_main, tn_target, 2 * LANES)
    in_specs = [
        pl.BlockSpec((tm, d), lambda i, j: (i, 0)),
        pl.BlockSpec((1, d), lambda i, j: (0, 0)),
        pl.BlockSpec((d, tn), lambda i, j: (0, j)),
        pl.BlockSpec((1, tn), lambda i, j: (0, j)),
    ]
    out_specs = [pl.BlockSpec((tm, tn), lambda i, j: (i, j))]
    out_shape = [jax.ShapeDtypeStruct((rows, n_main), BF16)]
    args = [h, nw.reshape(1, d), w, bias.reshape(1, -1)]
    if side is not None:
        w_side, b_side = side
        n_side = w_side.shape[1]
        in_specs += [pl.BlockSpec((d, n_side), lambda i, j: (0, 0)), pl.BlockSpec((1, n_side), lambda i, j: (0, 0))]
        out_specs += [pl.BlockSpec((tm, n_side), lambda i, j: (i, 0))]
        out_shape += [jax.ShapeDtypeStruct((rows, n_side), F32)]
        args += [w_side, b_side.reshape(1, n_side)]
    return pl.pallas_call(
        functools.partial(_norm_proj_kernel, has_side=side is not None),
        grid=(rows // tm, n_main // tn),
        in_specs=in_specs,
        out_specs=out_specs,
        out_shape=out_shape,
        scratch_shapes=[pltpu.VMEM((tm, d), BF16)],
        compiler_params=_params("parallel", "arbitrary"),
        name="norm_proj",
    )(*args)


def _proj_residual_kernel(a_ref, w_ref, b_ref, h_ref, o_ref, *, tm, rows_per_batch):
    i = pl.program_id(0)
    y = _dot(a_ref[...], w_ref[...]) + b_ref[...]
    row = lax.rem(i * tm, rows_per_batch) + lax.broadcasted_iota(jnp.int32, (tm, 1), 0)
    o_ref[...] = h_ref[...] + jnp.where(_is_token_row(row, rows_per_batch - SEQ_BLOCK), y, 0.0)


def _proj_residual(a, w, bias, h, rows_per_batch, *, tm_target=1056, tn=512):
    rows, k = a.shape
    d = h.shape[1]
    tm = _divisor_tile(rows_per_batch, tm_target, 16)
    kern = functools.partial(_proj_residual_kernel, tm=tm, rows_per_batch=rows_per_batch)
    return pl.pallas_call(
        kern,
        grid=(rows // tm, d // tn),
        in_specs=[
            pl.BlockSpec((tm, k), lambda i, j: (i, 0)),
            pl.BlockSpec((k, tn), lambda i, j: (0, j)),
            pl.BlockSpec((1, tn), lambda i, j: (0, j)),
            pl.BlockSpec((tm, tn), lambda i, j: (i, j)),
        ],
        out_specs=pl.BlockSpec((tm, tn), lambda i, j: (i, j)),
        out_shape=jax.ShapeDtypeStruct((rows, d), F32),
        compiler_params=_params("parallel", "arbitrary"),
        name="proj_residual",
    )(a, w, bias.reshape(1, d), h)


def _mlstm_kernel(q_ref, k_ref, v_ref, og_ref, g_ref, nw_ref, o_ref, c_ref, n_ref, m_ref):
    c = pl.program_id(1)
    blk = SEQ_BLOCK
    heads = range(MLSTM_HEADS)

    @pl.when(c == 0)
    def _():
        c_ref[...] = jnp.zeros_like(c_ref)
        n_ref[...] = jnp.zeros_like(n_ref)
        m_ref[...] = jnp.zeros_like(m_ref)

    gates = g_ref[...]
    row = c * blk + lax.broadcasted_iota(jnp.int32, (blk, 1), 0)
    valid = row >= LEAD_PAD
    log_f = jnp.where(valid, jnp.minimum(gates, 0.0) - jnp.log(1.0 + jnp.exp(-jnp.abs(gates))), 0.0)
    b_all = _cumsum_rows(log_f)
    li_all = jnp.where(valid, gates, -jnp.inf)

    ri = lax.broadcasted_iota(jnp.int32, (blk, blk), 0)
    ci = lax.broadcasted_iota(jnp.int32, (blk, blk), 1)
    eye = ri == ci
    causal = ci <= ri
    scale = MLSTM_DK ** -0.5

    q = {hd: q_ref[:, hd * MLSTM_DK:(hd + 1) * MLSTM_DK] for hd in heads}
    k = {hd: k_ref[:, hd * MLSTM_DK:(hd + 1) * MLSTM_DK] for hd in heads}
    v = {hd: v_ref[:, hd * MLSTM_DV:(hd + 1) * MLSTM_DV] for hd in heads}
    c_st = {hd: c_ref[hd] for hd in heads}
    n_st = {hd: n_ref[hd] for hd in heads}
    m_prev = {hd: m_ref[hd][:, :1] for hd in heads}

    qk_raw = {hd: _dot_nt(q[hd], k[hd]) for hd in heads}
    q_c = {hd: _dot(q[hd], c_st[hd].astype(BF16)) for hd in heads}

    b_c = {hd: b_all[:, MLSTM_HEADS + hd:MLSTM_HEADS + hd + 1] for hd in heads}
    d_c = {hd: li_all[:, hd:hd + 1] - b_c[hd] for hd in heads}
    b_last = {hd: b_c[hd][blk - 1:blk, :] for hd in heads}

    a_init, ka = {}, {}
    for hd in heads:
        log_end_init = b_last[hd] + m_prev[hd]
        log_end = b_last[hd] + d_c[hd]
        m_new = jnp.maximum(log_end_init, jnp.max(log_end, axis=0, keepdims=True))
        a_init[hd] = jnp.exp(log_end_init - m_new)
        ka[hd] = k[hd].astype(F32) * jnp.exp(log_end - m_new)
        m_ref[hd] = jnp.broadcast_to(m_new, (1, LANES))
    kv = {hd: _dot_tn(ka[hd].astype(BF16), v[hd]) for hd in heads}

    d_r = {hd: _column_to_row(d_c[hd], eye) for hd in heads}
    log_w = {hd: jnp.where(causal, b_c[hd] + d_r[hd], -jnp.inf) for hd in heads}
    log_init = {hd: b_c[hd] + m_prev[hd] for hd in heads}
    m_t = {hd: jnp.maximum(log_init[hd], jnp.max(log_w[hd], axis=-1, keepdims=True)) for hd in heads}
    w_init = {hd: jnp.exp(log_init[hd] - m_t[hd]) * scale for hd in heads}
    qk = {hd: qk_raw[hd] * (jnp.exp(log_w[hd] - m_t[hd]) * scale) for hd in heads}
    pv = {hd: _dot(qk[hd].astype(BF16), v[hd]) for hd in heads}
    qn = {hd: jnp.sum(q[hd].astype(F32) * n_st[hd], axis=-1, keepdims=True) for hd in heads}
    den = {hd: w_init[hd] * qn[hd] + jnp.sum(qk[hd], axis=-1, keepdims=True) for hd in heads}
    for hd in heads:
        hh = (w_init[hd] * q_c[hd] + pv[hd]) / jnp.maximum(jnp.abs(den[hd]), jnp.exp(-m_t[hd]))
        sl = slice(hd * MLSTM_DV, (hd + 1) * MLSTM_DV)
        gate = _sigmoid(og_ref[:, sl].astype(F32))
        o_ref[:, sl] = (_rms_normalize(hh) * nw_ref[:, sl] * gate).astype(o_ref.dtype)

    for hd in heads:
        c_ref[hd] = a_init[hd] * c_st[hd] + kv[hd]
        n_ref[hd] = a_init[hd] * n_st[hd] + jnp.sum(ka[hd], axis=0, keepdims=True)


def _mlstm_core(p_main, gates, norm_w, batch, rows_per_batch):
    blk = SEQ_BLOCK
    nblk = rows_per_batch // blk
    rows = p_main.shape[0]
    idx = lambda col: (lambda b, c: (_physical_block(b, c, nblk), col))
    return pl.pallas_call(
        _mlstm_kernel,
        grid=(batch, nblk),
        in_specs=[
            pl.BlockSpec((blk, MLSTM_QK_W), idx(0)),
            pl.BlockSpec((blk, MLSTM_QK_W), idx(1)),
            pl.BlockSpec((blk, MLSTM_V_W), idx(1)),
            pl.BlockSpec((blk, MLSTM_V_W), idx(2)),
            pl.BlockSpec((blk, 2 * MLSTM_HEADS), idx(0)),
            pl.BlockSpec((1, MLSTM_V_W), lambda b, c: (0, 0)),
        ],
        out_specs=pl.BlockSpec((blk, MLSTM_V_W), idx(0)),
        out_shape=jax.ShapeDtypeStruct((rows, MLSTM_V_W), BF16),
        scratch_shapes=[
            pltpu.VMEM((MLSTM_HEADS, MLSTM_DK, MLSTM_DV), F32),
            pltpu.VMEM((MLSTM_HEADS, 1, MLSTM_DK), F32),
            pltpu.VMEM((MLSTM_HEADS, 1, LANES), F32),
        ],
        compiler_params=_params("parallel", "arbitrary"),
        name="mlstm_core",
    )(p_main, p_main, p_main, p_main, gates, norm_w.reshape(1, MLSTM_V_W))


def _pool_kernel(h_ref, halo_ref, nw_ref, w_ref, sc_ref, o_ref, ext_ref, *, tm, rows_per_batch):
    i = pl.program_id(0)
    seq = rows_per_batch - SEQ_BLOCK
    x = h_ref[...]
    nw = nw_ref[...]
    u = _rms_normalize(x) * nw
    ext_ref[0:POOL_HALO, :] = _rms_normalize(halo_ref[...]) * nw
    ext_ref[POOL_HALO:, :] = u
    row = lax.rem(i * tm, rows_per_batch) + lax.broadcasted_iota(jnp.int32, (tm, 1), 0)
    pos = jnp.where(row < seq, row + N_META, row - (seq + LEAD_PAD))
    valid = pos >= 0
    for gi, win in enumerate(POOL_WINDOWS):
        sl = slice(gi * POOL_GROUP, (gi + 1) * POOL_GROUP)
        terms = [ext_ref[POOL_HALO - s:POOL_HALO - s + tm, sl] for s in range(win)]
        while len(terms) > 1:
            terms = [terms[a] + terms[a + 1] for a in range(0, len(terms), 2)]
        count = jnp.clip(pos + 1, 1, win).astype(F32)
        pooled = terms[0] / count - u[:, sl]
        y = _dot(pooled.astype(BF16), w_ref[gi]) * sc_ref[:, sl]
        o_ref[:, sl] = x[:, sl] + jnp.where(valid, y, 0.0)


def _pool_mixer(h, nw, w_group, scale, rows_per_batch, *, tm_target=384):
    rows, d = h.shape
    tm = _divisor_tile(rows_per_batch, tm_target, POOL_HALO)
    tpb = rows_per_batch // tm
    halo_per_tile = tm // POOL_HALO
    halo_per_batch = rows_per_batch // POOL_HALO

    def halo_index(i):
        return jnp.where(i % tpb == 0, (i // tpb + 1) * halo_per_batch - 1, i * halo_per_tile - 1), 0

    kern = functools.partial(_pool_kernel, tm=tm, rows_per_batch=rows_per_batch)
    return pl.pallas_call(
        kern,
        grid=(rows // tm,),
        in_specs=[
            pl.BlockSpec((tm, d), lambda i: (i, 0)),
            pl.BlockSpec((POOL_HALO, d), halo_index),
            pl.BlockSpec((1, d), lambda i: (0, 0)),
            pl.BlockSpec(w_group.shape, lambda i: (0, 0, 0)),
            pl.BlockSpec((1, d), lambda i: (0, 0)),
        ],
        out_specs=pl.BlockSpec((tm, d), lambda i: (i, 0)),
        out_shape=jax.ShapeDtypeStruct((rows, d), F32),
        scratch_shapes=[pltpu.VMEM((tm + POOL_HALO, d), F32)],
        compiler_params=_params("parallel"),
        name="pool_mixer",
    )(h, h, nw.reshape(1, d), w_group, scale.reshape(1, d))


GDN_CARRY = 8


def _unit_lower_inverses(strict_lowers):
    n = strict_lowers[0].shape[0]
    ri = lax.broadcasted_iota(jnp.int32, (n, n), 0)
    ci = lax.broadcasted_iota(jnp.int32, (n, n), 1)
    ident = jnp.where(ri == ci, 1.0, 0.0)
    powers = [-l for l in strict_lowers]
    invs = [ident + a for a in powers]
    terms = 2
    while terms < n:
        powers_b = [a.astype(BF16) for a in powers]
        powers = [_dot(ab, ab) for ab in powers_b]
        invs = [p + _dot(p.astype(BF16), a.astype(BF16)) for p, a in zip(invs, powers)]
        terms *= 2
    return invs


def _gdn_chunk(r0, gc_all, beta_all, s_ref, q_s, k_s, v_s, o_s, head_group):
    ch = GDN_CHUNK
    rows = slice(r0, r0 + ch)
    ri = lax.broadcasted_iota(jnp.int32, (ch, ch), 0)
    ci = lax.broadcasted_iota(jnp.int32, (ch, ch), 1)
    eye = ri == ci
    causal = ci <= ri
    strict = ci < ri
    rep = GDN_V_HEADS // GDN_QK_HEADS
    for g0 in range(0, GDN_V_HEADS, head_group):
        vheads = range(g0, g0 + head_group)
        qheads = range(g0 // rep, (g0 + head_group) // rep)
        q = {j: q_s[j, rows, :] for j in qheads}
        k = {j: k_s[j, rows, :] for j in qheads}
        kk = {j: _dot_nt(k[j], k[j]) for j in qheads}
        qk = {j: _dot_nt(q[j], k[j]) for j in qheads}
        gc_c = {jv: gc_all[:, jv:jv + 1] for jv in vheads}
        beta_c = {jv: beta_all[:, jv:jv + 1] for jv in vheads}
        decay = {jv: jnp.exp(jnp.where(causal, gc_c[jv] - _column_to_row(gc_c[jv], eye), -jnp.inf)) for jv in vheads}
        t_inv = _unit_lower_inverses([jnp.where(strict, kk[jv // rep] * decay[jv], 0.0) * beta_c[jv] for jv in vheads])
        t_inv = dict(zip(vheads, t_inv))
        egc = {jv: jnp.exp(gc_c[jv]) for jv in vheads}
        s_st = {jv: s_ref[jv] for jv in vheads}
        s_b = {jv: s_st[jv].astype(BF16) for jv in vheads}
        qs = {jv: _dot(q[jv // rep], s_b[jv]) * egc[jv] for jv in vheads}
        rhs = {jv: jnp.concatenate([v_s[jv, rows, :].astype(F32) * beta_c[jv],
                                    k[jv // rep].astype(F32) * (beta_c[jv] * egc[jv])], axis=-1).astype(BF16)
               for jv in vheads}
        sol = {jv: _dot(t_inv[jv].astype(BF16), rhs[jv]) for jv in vheads}
        v_new = {jv: sol[jv][:, :GDN_DV] - _dot(sol[jv][:, GDN_DV:].astype(BF16), s_b[jv]) for jv in vheads}
        for jv in vheads:
            o_s[jv, rows, :] = qs[jv] + _dot((qk[jv // rep] * decay[jv]).astype(BF16), v_new[jv].astype(BF16))
        for jv in vheads:
            g_last = gc_c[jv][ch - 1:ch, :]
            v_dec = (v_new[jv] * jnp.exp(g_last - gc_c[jv])).astype(BF16)
            s_ref[jv] = jnp.exp(g_last) * s_st[jv] + _dot_tn(k[jv // rep], v_dec)


def _gdn_kernel(x_ref, z_ref, e_ref, cw_ref, al_ref, dt_ref, nw_ref, o_ref,
                ext_ref, s_ref, q_s, k_s, v_s, o_s, *, head_group):
    c = pl.program_id(1)
    blk = SEQ_BLOCK

    @pl.when(c == 0)
    def _():
        ext_ref[0:GDN_CARRY, :] = jnp.zeros((GDN_CARRY, GDN_CONV_CH), F32)
        s_ref[...] = jnp.zeros_like(s_ref)

    ext_ref[GDN_CARRY:, :] = x_ref[...].astype(F32)
    n_cols = GDN_CONV_CH // LANES
    for cb in range(n_cols):
        sl = slice(cb * LANES, (cb + 1) * LANES)
        acc = None
        for tap in range(GDN_CONV):
            off = GDN_CARRY - (GDN_CONV - 1) + tap
            term = ext_ref[off:off + blk, sl] * cw_ref[tap:tap + 1, sl]
            acc = term if acc is None else acc + term
        y = acc * _sigmoid(acc)
        if cb < 2 * GDN_QK_HEADS:
            y = y * lax.rsqrt(jnp.sum(y * y, axis=-1, keepdims=True) + EPS)
            if cb < GDN_QK_HEADS:
                q_s[cb] = (y * (GDN_DK ** -0.5)).astype(BF16)
            else:
                k_s[cb - GDN_QK_HEADS] = y.astype(BF16)
        else:
            v_s[cb - 2 * GDN_QK_HEADS] = y.astype(BF16)
    ext_ref[0:GDN_CARRY, :] = ext_ref[blk:blk + GDN_CARRY, :]

    row = c * blk + lax.broadcasted_iota(jnp.int32, (blk, 1), 0)
    valid = row >= LEAD_PAD
    e = e_ref[...]
    beta_all = jnp.where(valid, _sigmoid(e[:, :GDN_V_HEADS]), 0.0)
    g_all = jnp.where(valid, -jnp.exp(al_ref[...]) * _softplus(e[:, GDN_V_HEADS:] + dt_ref[...]), 0.0)

    for r0 in range(0, blk, GDN_CHUNK):
        rows = slice(r0, r0 + GDN_CHUNK)
        _gdn_chunk(r0, _cumsum_rows(g_all[rows]), beta_all[rows], s_ref, q_s, k_s, v_s, o_s, head_group)

    for jv in range(GDN_V_HEADS):
        sl = slice(jv * GDN_DV, (jv + 1) * GDN_DV)
        z = z_ref[:, sl].astype(F32)
        o_ref[:, sl] = (_rms_normalize(o_s[jv]) * nw_ref[...] * (z * _sigmoid(z))).astype(o_ref.dtype)


def _gdn_core(p_main, side, conv_w, a_log, dt_bias, norm_w, batch, rows_per_batch, *, head_group=16):
    blk = SEQ_BLOCK
    nblk = rows_per_batch // blk
    rows = p_main.shape[0]
    idx = lambda col: (lambda b, c: (_physical_block(b, c, nblk), col))
    const2 = lambda b, c: (0, 0)
    kern = functools.partial(_gdn_kernel, head_group=head_group)
    return pl.pallas_call(
        kern,
        grid=(batch, nblk),
        in_specs=[
            pl.BlockSpec((blk, GDN_CONV_CH), idx(0)),
            pl.BlockSpec((blk, GDN_V_W), idx(GDN_CONV_CH // GDN_V_W)),
            pl.BlockSpec((blk, 2 * GDN_V_HEADS), idx(0)),
            pl.BlockSpec((GDN_CONV, GDN_CONV_CH), const2),
            pl.BlockSpec((1, GDN_V_HEADS), const2),
            pl.BlockSpec((1, GDN_V_HEADS), const2),
            pl.BlockSpec((1, GDN_DV), const2),
        ],
        out_specs=pl.BlockSpec((blk, GDN_V_W), idx(0)),
        out_shape=jax.ShapeDtypeStruct((rows, GDN_V_W), BF16),
        scratch_shapes=[
            pltpu.VMEM((blk + GDN_CARRY, GDN_CONV_CH), F32),
            pltpu.VMEM((GDN_V_HEADS, GDN_DK, GDN_DV), F32),
            pltpu.VMEM((GDN_QK_HEADS, blk, GDN_DK), BF16),
            pltpu.VMEM((GDN_QK_HEADS, blk, GDN_DK), BF16),
            pltpu.VMEM((GDN_V_HEADS, blk, GDN_DV), BF16),
            pltpu.VMEM((GDN_V_HEADS, blk, GDN_DV), F32),
        ],
        compiler_params=_params("parallel", "arbitrary"),
        name="gdn_core",
    )(p_main, p_main, side, conv_w, a_log.reshape(1, -1), dt_bias.reshape(1, -1), norm_w.reshape(1, -1))


def _rope_table_kernel(cos_ref, sin_ref):
    shape = cos_ref.shape
    pos = (lax.broadcasted_iota(jnp.int32, shape, 0) - LEAD_PAD).astype(F32)
    pair = lax.rem(lax.broadcasted_iota(jnp.int32, shape, 1), SWA_DH // 2).astype(F32)
    inv = jnp.exp(pair * (-2.0 * math.log(ROPE_THETA) / SWA_DH))
    ang = pos * inv
    cos_ref[...] = jnp.cos(ang)
    sin_ref[...] = jnp.sin(ang)


def _rope_tables(rows_per_batch):
    shape = jax.ShapeDtypeStruct((rows_per_batch, LANES), F32)
    return pl.pallas_call(_rope_table_kernel, out_shape=[shape, shape], name="rope_tables")()


SWA_COL_BATCH = 4


def _swa_kernel(sink_ref, q_ref, kv_ref, kvp_ref, cos_ref, sin_ref, cosp_ref, sinp_ref, o_ref):
    n = pl.program_id(1)
    blk = SEQ_BLOCK
    lane = lax.broadcasted_iota(jnp.int32, (1, LANES), 1)
    low = lane < SWA_DH
    first_half = lax.rem(lane, SWA_DH) < SWA_DH // 2

    def rope(x, cos, sin):
        rot = jnp.where(first_half, -pltpu.roll(x, LANES - SWA_DH // 2, 1), pltpu.roll(x, SWA_DH // 2, 1))
        return x * cos + rot * sin

    cos, sin = cos_ref[...], sin_ref[...]
    cosp, sinp = cosp_ref[...], sinp_ref[...]

    stacked = lax.broadcasted_iota(jnp.int32, (2 * blk, 2 * blk), 0)
    q_row = n * blk + jnp.where(stacked < blk, stacked, stacked - blk)
    k_row = (n - 1) * blk + lax.broadcasted_iota(jnp.int32, (2 * blk, 2 * blk), 1)
    mask = (k_row <= q_row) & (q_row - k_row < SWA_WINDOW) & (k_row >= LEAD_PAD)
    top = lax.broadcasted_iota(jnp.int32, (2 * blk, 1), 0) < blk

    def both_heads(x, head_in_pair):
        swapped = pltpu.roll(x, SWA_DH, 1)
        return jnp.where(low, x, swapped) if head_in_pair == 0 else jnp.where(low, swapped, x)

    kk, v_stack = {}, {}
    for g in range(SWA_HKV):
        pair_col, head_in_pair = divmod(g, 2)
        ksl = slice(pair_col * LANES, (pair_col + 1) * LANES)
        vsl = slice(SWA_KV_W + pair_col * LANES, SWA_KV_W + (pair_col + 1) * LANES)
        k_cur = both_heads(rope(kv_ref[:, ksl].astype(F32), cos, sin), head_in_pair)
        k_prev = both_heads(rope(kvp_ref[:, ksl].astype(F32), cosp, sinp), head_in_pair)
        kk[g] = jnp.concatenate([k_prev, k_cur], axis=0).astype(BF16)
        v_cur = both_heads(kv_ref[:, vsl].astype(F32), head_in_pair)
        v_prev = both_heads(kvp_ref[:, vsl].astype(F32), head_in_pair)
        v2 = jnp.concatenate([v_prev, v_cur], axis=0)
        v_stack[g] = jnp.concatenate([jnp.where(low, v2, 0.0), jnp.where(low, 0.0, v2)], axis=0).astype(BF16)

    def scores(col):
        qp = rope(q_ref[:, col * LANES:(col + 1) * LANES].astype(F32), cos, sin)
        q2 = jnp.concatenate([jnp.where(low, qp, 0.0), jnp.where(low, 0.0, qp)], axis=0).astype(BF16)
        s = _dot_nt(q2, kk[col // (SWA_GROUP // 2)]) * (SWA_DH ** -0.5)
        return jnp.where(mask, s, -jnp.inf)

    def attend(cols, s):
        sink = {c: jnp.where(top, sink_ref[2 * c], sink_ref[2 * c + 1]) for c in cols}
        m = {c: jnp.maximum(jnp.max(s[c], axis=-1, keepdims=True), sink[c]) for c in cols}
        e = {c: jnp.exp(s[c] - m[c]) for c in cols}
        inv = {c: 1.0 / (jnp.sum(e[c], axis=-1, keepdims=True) + jnp.exp(sink[c] - m[c])) for c in cols}
        for c in cols:
            prob = e[c] * inv[c]
            p2 = jnp.concatenate([prob[:blk], prob[blk:]], axis=1).astype(BF16)
            o_ref[:, c * LANES:(c + 1) * LANES] = _dot(p2, v_stack[c // (SWA_GROUP // 2)]).astype(o_ref.dtype)

    n_cols = SWA_HQ // 2
    batches = [range(c0, c0 + SWA_COL_BATCH) for c0 in range(0, n_cols, SWA_COL_BATCH)]
    pending = {c: scores(c) for c in batches[0]}
    for bi, cols in enumerate(batches):
        current = {c: pending.pop(c) for c in cols}
        if bi + 1 < len(batches):
            pending.update({c: scores(c) for c in batches[bi + 1]})
        attend(cols, current)


def _swa_core(p_main, sinks, cos, sin, batch, rows_per_batch):
    blk = SEQ_BLOCK
    nblk = rows_per_batch // blk
    rows = p_main.shape[0]
    kv_col = SWA_Q_W // (2 * SWA_KV_W)
    cur = lambda col: (lambda b, n: (_physical_block(b, n, nblk), col))
    prev = lambda col: (lambda b, n: (_physical_block(b, jnp.maximum(n - 1, 0), nblk), col))
    tab_cur = lambda b, n: (n, 0)
    tab_prev = lambda b, n: (jnp.maximum(n - 1, 0), 0)
    return pl.pallas_call(
        _swa_kernel,
        grid=(batch, nblk),
        in_specs=[
            pl.BlockSpec(memory_space=pltpu.SMEM),
            pl.BlockSpec((blk, SWA_Q_W), cur(0)),
            pl.BlockSpec((blk, 2 * SWA_KV_W), cur(kv_col)),
            pl.BlockSpec((blk, 2 * SWA_KV_W), prev(kv_col)),
            pl.BlockSpec((blk, LANES), tab_cur),
            pl.BlockSpec((blk, LANES), tab_cur),
            pl.BlockSpec((blk, LANES), tab_prev),
            pl.BlockSpec((blk, LANES), tab_prev),
        ],
        out_specs=pl.BlockSpec((blk, SWA_Q_W), cur(0)),
        out_shape=jax.ShapeDtypeStruct((rows, SWA_Q_W), BF16),
        compiler_params=_params("parallel", "parallel"),
        name="swa_core",
    )(sinks, p_main, p_main, p_main, cos, sin, cos, sin)


def kernel(x, meta_tokens, norm_w, ffn_w_gate, ffn_w_up, ffn_w_down, mlstm_w_in, mlstm_b_if, mlstm_norm_w, mlstm_w_out, pool_w, pool_scale, gdn_w_in, gdn_conv_w, gdn_a_log, gdn_dt_bias, gdn_norm_w, gdn_w_out, swa_w_qkv, swa_b_qkv, swa_sinks, swa_w_out, swa_b_out, final_norm_w):
    batch, seq, d = x.shape
    rpb = _rows_per_batch(seq)
    zeros_d = jnp.zeros((d,), F32)
    (mlstm_w_in, mlstm_w_out, pool_w, gdn_w_in, gdn_w_out, swa_w_qkv, swa_w_out) = (
        w.astype(BF16) for w in (mlstm_w_in, mlstm_w_out, pool_w, gdn_w_in, gdn_w_out, swa_w_qkv, swa_w_out))

    h = x
    for i in range(DEPTH):
        m, j = i % N_MIXERS, i // N_MIXERS
        h = _ffn(h, norm_w[i, 0], ffn_w_gate, ffn_w_up, ffn_w_down, i, 0, rpb,
                 meta=meta_tokens.astype(x.dtype) if i == 0 else None)
        h = h.reshape(batch * rpb, d)
        if m == 0:
            p_main, gates = _norm_proj(h, norm_w[i, 1], mlstm_w_in[j], jnp.zeros((MLSTM_MAIN_W,), F32),
                                       MLSTM_MAIN_W, (mlstm_w_in[j][:, MLSTM_MAIN_W:], mlstm_b_if[j]))
            a = _mlstm_core(p_main, gates, mlstm_norm_w[j], batch, rpb)
            h = _proj_residual(a, mlstm_w_out[j], zeros_d, h, rpb)
        elif m == 1:
            h = _pool_mixer(h, norm_w[i, 1], pool_w[j], pool_scale[j], rpb)
        elif m == 2:
            p_main, side = _norm_proj(h, norm_w[i, 1], gdn_w_in[j], jnp.zeros((GDN_MAIN_W,), F32), GDN_MAIN_W,
                                      (gdn_w_in[j][:, GDN_MAIN_W:], jnp.zeros((2 * GDN_V_HEADS,), F32)))
            a = _gdn_core(p_main, side, gdn_conv_w[j], gdn_a_log[j], gdn_dt_bias[j], gdn_norm_w[j], batch, rpb)
            h = _proj_residual(a, gdn_w_out[j], zeros_d, h, rpb)
        else:
            (p_main,) = _norm_proj(h, norm_w[i, 1], swa_w_qkv[j], swa_b_qkv[j], SWA_IN)
            cos, sin = _rope_tables(rpb)
            a = _swa_core(p_main, swa_sinks[j], cos, sin, batch, rpb)
            h = _proj_residual(a, swa_w_out[j], swa_b_out[j], h, rpb)
        h = _ffn(h.reshape(batch, rpb, d), norm_w[i, 2], ffn_w_gate, ffn_w_up, ffn_w_down, i, 1, rpb,
                 final_w=final_norm_w if i == DEPTH - 1 else None)
    return h
```

```python
import functools
import math

import jax
import jax.numpy as jnp
from jax import lax
from jax.experimental import pallas as pl
from jax.experimental.pallas import tpu as pltpu

F32 = jnp.float32
BF16 = jnp.bfloat16

D_MODEL = 2048
DEPTH = 4
N_MIXERS = 4
N_META = 16
EPS = 1e-6

MLSTM_HEADS = 8
MLSTM_DV = 256
MLSTM_DK = 128
MLSTM_QK_W = MLSTM_HEADS * MLSTM_DK
MLSTM_V_W = MLSTM_HEADS * MLSTM_DV
MLSTM_MAIN_W = 2 * MLSTM_QK_W + 2 * MLSTM_V_W

POOL_WINDOWS = (2, 4, 8, 16)
POOL_GROUP = D_MODEL // len(POOL_WINDOWS)
POOL_HALO = 16

GDN_DK = 128
GDN_DV = 128
GDN_QK_HEADS = 16
GDN_V_HEADS = 32
GDN_CONV = 4
GDN_CHUNK = 64
GDN_QK_W = GDN_QK_HEADS * GDN_DK
GDN_V_W = GDN_V_HEADS * GDN_DV
GDN_CONV_CH = 2 * GDN_QK_W + GDN_V_W
GDN_MAIN_W = GDN_CONV_CH + GDN_V_W

SWA_DH = 64
SWA_HQ = 32
SWA_GROUP = 8
SWA_HKV = 4
SWA_WINDOW = 128
SWA_Q_W = SWA_HQ * SWA_DH
SWA_KV_W = SWA_HKV * SWA_DH
SWA_IN = SWA_Q_W + 2 * SWA_KV_W
ROPE_THETA = 10000.0

LANES = 128
SEQ_BLOCK = 128
LEAD_PAD = SEQ_BLOCK - N_META

VMEM_LIMIT_BYTES = 56 * 1024 * 1024
FFN_PROLOGUE_ROWS = 384


def _rows_per_batch(seq):
    assert seq % SEQ_BLOCK == 0, seq
    return seq + SEQ_BLOCK


def _divisor_tile(total, target, multiple):
    best = None
    for t in range(multiple, min(total, target) + 1, multiple):
        if total % t == 0:
            best = t
    assert best is not None, (total, target, multiple)
    return best


def _params(*semantics):
    return pltpu.CompilerParams(dimension_semantics=semantics, vmem_limit_bytes=VMEM_LIMIT_BYTES)


def _physical_block(b, n, nblk):
    return b * nblk + jnp.where(n == 0, nblk - 1, n - 1)


def _is_token_row(row_in_batch, seq):
    return (row_in_batch < seq) | (row_in_batch >= seq + LEAD_PAD)


def _rms_normalize(x):
    return x * lax.rsqrt(jnp.mean(x * x, axis=-1, keepdims=True) + EPS)


def _sigmoid(x):
    return 1.0 / (1.0 + jnp.exp(-x))


def _softplus(x):
    return jnp.maximum(x, 0.0) + jnp.log(1.0 + jnp.exp(-jnp.abs(x)))


def _dot(a, b):
    return jnp.dot(a, b, preferred_element_type=F32)


def _dot_nt(a, b):
    return lax.dot_general(a, b, (((1,), (1,)), ((), ())), preferred_element_type=F32)


def _dot_tn(a, b):
    return lax.dot_general(a, b, (((0,), (0,)), ((), ())), preferred_element_type=F32)


def _cumsum_rows(x):
    n = x.shape[0]
    row = lax.broadcasted_iota(jnp.int32, x.shape, 0)
    shift = 1
    while shift < n:
        x = x + jnp.where(row >= shift, pltpu.roll(x, shift, 0), 0.0)
        shift *= 2
    return x


def _column_to_row(col, eye):
    return jnp.sum(jnp.where(eye, col, 0.0), axis=0, keepdims=True)


def _ffn_kernel(*refs, assemble, final_norm, tiles_per_batch):
    h_ref, nw_ref, wg_ref, wu_ref, wd_ref = refs[:5]
    rest = list(refs[5:])
    meta_ref = rest.pop(0) if assemble else None
    fw_ref = rest.pop(0) if final_norm else None
    o_ref, xn_ref = rest
    j = pl.program_id(1)

    tm = o_ref.shape[0]
    sub = _divisor_tile(tm, FFN_PROLOGUE_ROWS, 16)
    assert sub >= SEQ_BLOCK

    def swiglu_rows(r0, r1, wg, wu, wd):
        xn = xn_ref[r0:r1, :]
        g = _dot(xn, wg)
        u = _dot(xn, wu)
        a = (0.5 * g * _sigmoid(g) * u).astype(BF16)
        o_ref[r0:r1, :] += _dot(a, wd)

    @pl.when(j == 0)
    def _():
        body_end = tm
        if assemble:
            body_end = tm - SEQ_BLOCK
            ends_batch = lax.rem(pl.program_id(0), tiles_per_batch) == tiles_per_batch - 1

            @pl.when(ends_batch)
            def _():
                o_ref[body_end:tm - N_META, :] = jnp.zeros((LEAD_PAD, o_ref.shape[1]), F32)
                o_ref[tm - N_META:, :] = meta_ref[...]

            @pl.when(jnp.logical_not(ends_batch))
            def _():
                o_ref[body_end:, :] = h_ref[body_end:, :]

        wg, wu, wd = (w[...].astype(BF16) for w in (wg_ref, wu_ref, wd_ref))
        for r0 in range(0, tm, sub):
            r1 = r0 + sub
            rc = min(r1, body_end)
            if rc > r0:
                o_ref[r0:rc, :] = h_ref[r0:rc, :]
            xn_ref[r0:r1, :] = (_rms_normalize(o_ref[r0:r1, :]) * nw_ref[...]).astype(BF16)
            swiglu_rows(r0, r1, wg, wu, wd)

    @pl.when(j > 0)
    def _():
        swiglu_rows(0, tm, *(w[...].astype(BF16) for w in (wg_ref, wu_ref, wd_ref)))

    if final_norm:
        @pl.when(j == pl.num_programs(1) - 1)
        def _():
            o_ref[...] = _rms_normalize(o_ref[...]) * fw_ref[...]


def _ffn(h, nw, w_gate, w_up, w_down, layer, half, rows_per_batch, *, meta=None, final_w=None,
         tm_target=1056, tf=256):
    batch, _, d = h.shape
    d_ff = w_gate.shape[-1]
    tm = _divisor_tile(rows_per_batch, tm_target, SEQ_BLOCK // 8)
    assert tm >= SEQ_BLOCK and d_ff % tf == 0
    tpb = rows_per_batch // tm
    row_spec = pl.BlockSpec((None, tm, d), lambda i, j: (i // tpb, i % tpb, 0))
    vec_spec = pl.BlockSpec((1, d), lambda i, j: (0, 0))
    in_specs = [
        row_spec, vec_spec,
        pl.BlockSpec((None, None, d, tf), lambda i, j: (layer, half, 0, j)),
        pl.BlockSpec((None, None, d, tf), lambda i, j: (layer, half, 0, j)),
        pl.BlockSpec((None, None, tf, d), lambda i, j: (layer, half, j, 0)),
    ]
    args = [h, nw.reshape(1, d), w_gate, w_up, w_down]
    if meta is not None:
        in_specs.append(pl.BlockSpec((N_META, d), lambda i, j: (0, 0)))
        args.append(meta)
    if final_w is not None:
        in_specs.append(vec_spec)
        args.append(final_w.reshape(1, d))
    out_rows = rows_per_batch - SEQ_BLOCK if final_w is not None else rows_per_batch
    kern = functools.partial(_ffn_kernel, assemble=meta is not None, final_norm=final_w is not None,
                             tiles_per_batch=tpb)
    return pl.pallas_call(
        kern,
        grid=(batch * tpb, d_ff // tf),
        in_specs=in_specs,
        out_specs=row_spec,
        out_shape=jax.ShapeDtypeStruct((batch, out_rows, d), F32),
        scratch_shapes=[pltpu.VMEM((tm, d), BF16)],
        compiler_params=_params("parallel", "arbitrary"),
        name="ffn",
    )(*args)


def _norm_proj_kernel(*refs, has_side):
    if has_side:
        h_ref, nw_ref, w_ref, b_ref, ws_ref, bs_ref, o_ref, os_ref, xn_ref = refs
    else:
        h_ref, nw_ref, w_ref, b_ref, o_ref, xn_ref = refs
    j = pl.program_id(1)
    tm = o_ref.shape[0]
    sub = _divisor_tile(tm, FFN_PROLOGUE_ROWS, 16)

    @pl.when(j == 0)
    def _():
        w = w_ref[...].astype(BF16)
        ws = ws_ref[...].astype(BF16) if has_side else None
        for r0 in range(0, tm, sub):
            rows = slice(r0, r0 + sub)
            xn = (_rms_normalize(h_ref[rows, :]) * nw_ref[...]).astype(BF16)
            xn_ref[rows, :] = xn
            o_ref[rows, :] = (_dot(xn, w) + b_ref[...]).astype(o_ref.dtype)
            if has_side:
                os_ref[rows, :] = _dot(xn, ws) + bs_ref[...]

    @pl.when(j > 0)
    def _():
        o_ref[...] = (_dot(xn_ref[...], w_ref[...].astype(BF16)) + b_ref[...]).astype(o_ref.dtype)


def _norm_proj(h, nw, w, bias, n_main, side=None, *, tm_target=1056, tn_target=1024):
    rows, d = h.shape
    tm = _divisor_tile(rows, tm_target, 16)
    tn = _divisor_tile(n_main, tn_target, 2 * LANES)
    in_specs = [
        pl.BlockSpec((tm, d), lambda i, j: (i, 0)),
        pl.BlockSpec((1, d), lambda i, j: (0, 0)),
        pl.BlockSpec((d, tn), lambda i, j: (0, j)),
        pl.BlockSpec((1, tn), lambda i, j: (0, j)),
    ]
    out_specs = [pl.BlockSpec((tm, tn), lambda i, j: (i, j))]
    out_shape = [jax.ShapeDtypeStruct((rows, n_main), BF16)]
    args = [h, nw.reshape(1, d), w, bias.reshape(1, -1)]
    if side is not None:
        w_side, b_side = side
        n_side = w_side.shape[1]
        in_specs += [pl.BlockSpec((d, n_side), lambda i, j: (0, 0)), pl.BlockSpec((1, n_side), lambda i, j: (0, 0))]
        out_specs += [pl.BlockSpec((tm, n_side), lambda i, j: (i, 0))]
        out_shape += [jax.ShapeDtypeStruct((rows, n_side), F32)]
        args += [w_side, b_side.reshape(1, n_side)]
    return pl.pallas_call(
        functools.partial(_norm_proj_kernel, has_side=side is not None),
        grid=(rows // tm, n_main // tn),
        in_specs=in_specs,
        out_specs=out_specs,
        out_shape=out_shape,
        scratch_shapes=[pltpu.VMEM((tm, d), BF16)],
        compiler_params=_params("parallel", "arbitrary"),
        name="norm_proj",
    )(*args)


def _proj_residual_kernel(a_ref, w_ref, b_ref, h_ref, o_ref, *, tm, rows_per_batch):
    i = pl.program_id(0)
    y = _dot(a_ref[...], w_ref[...].astype(BF16)) + b_ref[...]
    row = lax.rem(i * tm, rows_per_batch) + lax.broadcasted_iota(jnp.int32, (tm, 1), 0)
    o_ref[...] = h_ref[...] + jnp.where(_is_token_row(row, rows_per_batch - SEQ_BLOCK), y, 0.0)


def _proj_residual(a, w, bias, h, rows_per_batch, *, tm_target=1056, tn=512):
    rows, k = a.shape
    d = h.shape[1]
    tm = _divisor_tile(rows_per_batch, tm_target, 16)
    kern = functools.partial(_proj_residual_kernel, tm=tm, rows_per_batch=rows_per_batch)
    return pl.pallas_call(
        kern,
        grid=(rows // tm, d // tn),
        in_specs=[
            pl.BlockSpec((tm, k), lambda i, j: (i, 0)),
            pl.BlockSpec((k, tn), lambda i, j: (0, j)),
            pl.BlockSpec((1, tn), lambda i, j: (0, j)),
            pl.BlockSpec((tm, tn), lambda i, j: (i, j)),
        ],
        out_specs=pl.BlockSpec((tm, tn), lambda i, j: (i, j)),
        out_shape=jax.ShapeDtypeStruct((rows, d), F32),
        compiler_params=_params("parallel", "arbitrary"),
        name="proj_residual",
    )(a, w, bias.reshape(1, d), h)


def _mlstm_kernel(q_ref, k_ref, v_ref, og_ref, g_ref, nw_ref, o_ref, c_ref, n_ref, m_ref):
    c = pl.program_id(1)
    blk = SEQ_BLOCK
    heads = range(MLSTM_HEADS)

    @pl.when(c == 0)
    def _():
        c_ref[...] = jnp.zeros_like(c_ref)
        n_ref[...] = jnp.zeros_like(n_ref)
        m_ref[...] = jnp.zeros_like(m_ref)

    gates = g_ref[...]
    row = c * blk + lax.broadcasted_iota(jnp.int32, (blk, 1), 0)
    valid = row >= LEAD_PAD
    log_f = jnp.where(valid, jnp.minimum(gates, 0.0) - jnp.log(1.0 + jnp.exp(-jnp.abs(gates))), 0.0)
    b_all = _cumsum_rows(log_f)
    li_all = jnp.where(valid, gates, -jnp.inf)

    ri = lax.broadcasted_iota(jnp.int32, (blk, blk), 0)
    ci = lax.broadcasted_iota(jnp.int32, (blk, blk), 1)
    eye = ri == ci
    causal = ci <= ri
    scale = MLSTM_DK ** -0.5

    q = {hd: q_ref[:, hd * MLSTM_DK:(hd + 1) * MLSTM_DK] for hd in heads}
    k = {hd: k_ref[:, hd * MLSTM_DK:(hd + 1) * MLSTM_DK] for hd in heads}
    v = {hd: v_ref[:, hd * MLSTM_DV:(hd + 1) * MLSTM_DV] for hd in heads}
    c_st = {hd: c_ref[hd] for hd in heads}
    n_st = {hd: n_ref[hd] for hd in heads}
    m_prev = {hd: m_ref[hd][:, :1] for hd in heads}

    qk_raw = {hd: _dot_nt(q[hd], k[hd]) for hd in heads}
    q_c = {hd: _dot(q[hd], c_st[hd].astype(BF16)) for hd in heads}

    b_c = {hd: b_all[:, MLSTM_HEADS + hd:MLSTM_HEADS + hd + 1] for hd in heads}
    d_c = {hd: li_all[:, hd:hd + 1] - b_c[hd] for hd in heads}
    b_last = {hd: b_c[hd][blk - 1:blk, :] for hd in heads}

    a_init, ka = {}, {}
    for hd in heads:
        log_end_init = b_last[hd] + m_prev[hd]
        log_end = b_last[hd] + d_c[hd]
        m_new = jnp.maximum(log_end_init, jnp.max(log_end, axis=0, keepdims=True))
        a_init[hd] = jnp.exp(log_end_init - m_new)
        ka[hd] = k[hd].astype(F32) * jnp.exp(log_end - m_new)
        m_ref[hd] = jnp.broadcast_to(m_new, (1, LANES))
    kv = {hd: _dot_tn(ka[hd].astype(BF16), v[hd]) for hd in heads}

    d_r = {hd: _column_to_row(d_c[hd], eye) for hd in heads}
    log_w = {hd: jnp.where(causal, b_c[hd] + d_r[hd], -jnp.inf) for hd in heads}
    log_init = {hd: b_c[hd] + m_prev[hd] for hd in heads}
    m_t = {hd: jnp.maximum(log_init[hd], jnp.max(log_w[hd], axis=-1, keepdims=True)) for hd in heads}
    w_init = {hd: jnp.exp(log_init[hd] - m_t[hd]) * scale for hd in heads}
    qk = {hd: qk_raw[hd] * (jnp.exp(log_w[hd] - m_t[hd]) * scale) for hd in heads}
    pv = {hd: _dot(qk[hd].astype(BF16), v[hd]) for hd in heads}
    qn = {hd: jnp.sum(q[hd].astype(F32) * n_st[hd], axis=-1, keepdims=True) for hd in heads}
    den = {hd: w_init[hd] * qn[hd] + jnp.sum(qk[hd], axis=-1, keepdims=True) for hd in heads}
    for hd in heads:
        hh = (w_init[hd] * q_c[hd] + pv[hd]) / jnp.maximum(jnp.abs(den[hd]), jnp.exp(-m_t[hd]))
        sl = slice(hd * MLSTM_DV, (hd + 1) * MLSTM_DV)
        gate = _sigmoid(og_ref[:, sl].astype(F32))
        o_ref[:, sl] = (_rms_normalize(hh) * nw_ref[:, sl] * gate).astype(o_ref.dtype)

    for hd in heads:
        c_ref[hd] = a_init[hd] * c_st[hd] + kv[hd]
        n_ref[hd] = a_init[hd] * n_st[hd] + jnp.sum(ka[hd], axis=0, keepdims=True)


def _mlstm_core(p_main, gates, norm_w, batch, rows_per_batch):
    blk = SEQ_BLOCK
    nblk = rows_per_batch // blk
    rows = p_main.shape[0]
    idx = lambda col: (lambda b, c: (_physical_block(b, c, nblk), col))
    return pl.pallas_call(
        _mlstm_kernel,
        grid=(batch, nblk),
        in_specs=[
            pl.BlockSpec((blk, MLSTM_QK_W), idx(0)),
            pl.BlockSpec((blk, MLSTM_QK_W), idx(1)),
            pl.BlockSpec((blk, MLSTM_V_W), idx(1)),
            pl.BlockSpec((blk, MLSTM_V_W), idx(2)),
            pl.BlockSpec((blk, 2 * MLSTM_HEADS), idx(0)),
            pl.BlockSpec((1, MLSTM_V_W), lambda b, c: (0, 0)),
        ],
        out_specs=pl.BlockSpec((blk, MLSTM_V_W), idx(0)),
        out_shape=jax.ShapeDtypeStruct((rows, MLSTM_V_W), BF16),
        scratch_shapes=[
            pltpu.VMEM((MLSTM_HEADS, MLSTM_DK, MLSTM_DV), F32),
            pltpu.VMEM((MLSTM_HEADS, 1, MLSTM_DK), F32),
            pltpu.VMEM((MLSTM_HEADS, 1, LANES), F32),
        ],
        compiler_params=_params("parallel", "arbitrary"),
        name="mlstm_core",
    )(p_main, p_main, p_main, p_main, gates, norm_w.reshape(1, MLSTM_V_W))


def _pool_kernel(h_ref, halo_ref, nw_ref, w_ref, sc_ref, o_ref, ext_ref, *, tm, rows_per_batch):
    i = pl.program_id(0)
    seq = rows_per_batch - SEQ_BLOCK
    x = h_ref[...]
    nw = nw_ref[...]
    u = _rms_normalize(x) * nw
    ext_ref[0:POOL_HALO, :] = _rms_normalize(halo_ref[...]) * nw
    ext_ref[POOL_HALO:, :] = u
    row = lax.rem(i * tm, rows_per_batch) + lax.broadcasted_iota(jnp.int32, (tm, 1), 0)
    pos = jnp.where(row < seq, row + N_META, row - (seq + LEAD_PAD))
    valid = pos >= 0
    for gi, win in enumerate(POOL_WINDOWS):
        sl = slice(gi * POOL_GROUP, (gi + 1) * POOL_GROUP)
        terms = [ext_ref[POOL_HALO - s:POOL_HALO - s + tm, sl] for s in range(win)]
        while len(terms) > 1:
            terms = [terms[a] + terms[a + 1] for a in range(0, len(terms), 2)]
        count = jnp.clip(pos + 1, 1, win).astype(F32)
        pooled = terms[0] / count - u[:, sl]
        y = _dot(pooled.astype(BF16), w_ref[gi].astype(BF16)) * sc_ref[:, sl]
        o_ref[:, sl] = x[:, sl] + jnp.where(valid, y, 0.0)


def _pool_mixer(h, nw, w_group, scale, rows_per_batch, *, tm_target=384):
    rows, d = h.shape
    tm = _divisor_tile(rows_per_batch, tm_target, POOL_HALO)
    tpb = rows_per_batch // tm
    halo_per_tile = tm // POOL_HALO
    halo_per_batch = rows_per_batch // POOL_HALO

    def halo_index(i):
        return jnp.where(i % tpb == 0, (i // tpb + 1) * halo_per_batch - 1, i * halo_per_tile - 1), 0

    kern = functools.partial(_pool_kernel, tm=tm, rows_per_batch=rows_per_batch)
    return pl.pallas_call(
        kern,
        grid=(rows // tm,),
        in_specs=[
            pl.BlockSpec((tm, d), lambda i: (i, 0)),
            pl.BlockSpec((POOL_HALO, d), halo_index),
            pl.BlockSpec((1, d), lambda i: (0, 0)),
            pl.BlockSpec(w_group.shape, lambda i: (0, 0, 0)),
            pl.BlockSpec((1, d), lambda i: (0, 0)),
        ],
        out_specs=pl.BlockSpec((tm, d), lambda i: (i, 0)),
        out_shape=jax.ShapeDtypeStruct((rows, d), F32),
        scratch_shapes=[pltpu.VMEM((tm + POOL_HALO, d), F32)],
        compiler_params=_params("parallel"),
        name="pool_mixer",
    )(h, h, nw.reshape(1, d), w_group, scale.reshape(1, d))


GDN_CARRY = 8


def _unit_lower_inverses(strict_lowers):
    n = strict_lowers[0].shape[0]
    ri = lax.broadcasted_iota(jnp.int32, (n, n), 0)
    ci = lax.broadcasted_iota(jnp.int32, (n, n), 1)
    ident = jnp.where(ri == ci, 1.0, 0.0)
    invs = [ident - l for l in strict_lowers]
    powers_b = [l.astype(BF16) for l in strict_lowers]
    terms = 2
    while terms < n:
        powers_b = [_dot(ab, ab).astype(BF16) for ab in powers_b]
        invs = [p + _dot(p.astype(BF16), ab) for p, ab in zip(invs, powers_b)]
        terms *= 2
    return invs


def _gdn_chunk(r0, gc_all, beta_all, s_ref, q_s, k_s, v_s, o_s, head_group):
    ch = GDN_CHUNK
    rows = slice(r0, r0 + ch)
    ri = lax.broadcasted_iota(jnp.int32, (ch, ch), 0)
    ci = lax.broadcasted_iota(jnp.int32, (ch, ch), 1)
    eye = ri == ci
    causal = ci <= ri
    strict = ci < ri
    rep = GDN_V_HEADS // GDN_QK_HEADS
    for g0 in range(0, GDN_V_HEADS, head_group):
        vheads = range(g0, g0 + head_group)
        qheads = range(g0 // rep, (g0 + head_group) // rep)
        q = {j: q_s[j, rows, :] for j in qheads}
        k = {j: k_s[j, rows, :] for j in qheads}
        kk = {j: _dot_nt(k[j], k[j]) for j in qheads}
        qk = {j: _dot_nt(q[j], k[j]) for j in qheads}
        gc_c = {jv: gc_all[:, jv:jv + 1] for jv in vheads}
        beta_c = {jv: beta_all[:, jv:jv + 1] for jv in vheads}
        decay = {jv: jnp.exp(jnp.where(causal, gc_c[jv] - _column_to_row(gc_c[jv], eye), -jnp.inf)) for jv in vheads}
        t_inv = _unit_lower_inverses([jnp.where(strict, kk[jv // rep] * decay[jv], 0.0) * beta_c[jv] for jv in vheads])
        t_inv = dict(zip(vheads, t_inv))
        egc = {jv: jnp.exp(gc_c[jv]) for jv in vheads}
        s_st = {jv: s_ref[jv] for jv in vheads}
        s_b = {jv: s_st[jv].astype(BF16) for jv in vheads}
        qs = {jv: _dot(q[jv // rep], s_b[jv]) * egc[jv] for jv in vheads}
        rhs = {jv: jnp.concatenate([v_s[jv, rows, :].astype(F32) * beta_c[jv],
                                    k[jv // rep].astype(F32) * (beta_c[jv] * egc[jv])], axis=-1).astype(BF16)
               for jv in vheads}
        sol = {jv: _dot(t_inv[jv].astype(BF16), rhs[jv]) for jv in vheads}
        v_new = {jv: sol[jv][:, :GDN_DV] - _dot(sol[jv][:, GDN_DV:].astype(BF16), s_b[jv]) for jv in vheads}
        for jv in vheads:
            o_s[jv, rows, :] = qs[jv] + _dot((qk[jv // rep] * decay[jv]).astype(BF16), v_new[jv].astype(BF16))
        for jv in vheads:
            g_last = gc_c[jv][ch - 1:ch, :]
            v_dec = (v_new[jv] * jnp.exp(g_last - gc_c[jv])).astype(BF16)
            s_ref[jv] = jnp.exp(g_last) * s_st[jv] + _dot_tn(k[jv // rep], v_dec)


def _gdn_kernel(x_ref, z_ref, e_ref, cw_ref, al_ref, dt_ref, nw_ref, o_ref,
                ext_ref, s_ref, q_s, k_s, v_s, o_s, *, head_group):
    c = pl.program_id(1)
    blk = SEQ_BLOCK

    @pl.when(c == 0)
    def _():
        ext_ref[0:GDN_CARRY, :] = jnp.zeros((GDN_CARRY, GDN_CONV_CH), F32)
        s_ref[...] = jnp.zeros_like(s_ref)

    ext_ref[GDN_CARRY:, :] = x_ref[...].astype(F32)
    n_cols = GDN_CONV_CH // LANES
    for cb in range(n_cols):
        sl = slice(cb * LANES, (cb + 1) * LANES)
        acc = None
        for tap in range(GDN_CONV):
            off = GDN_CARRY - (GDN_CONV - 1) + tap
            term = ext_ref[off:off + blk, sl] * cw_ref[tap:tap + 1, sl]
            acc = term if acc is None else acc + term
        y = acc * _sigmoid(acc)
        if cb < 2 * GDN_QK_HEADS:
            y = y * lax.rsqrt(jnp.sum(y * y, axis=-1, keepdims=True) + EPS)
            if cb < GDN_QK_HEADS:
                q_s[cb] = (y * (GDN_DK ** -0.5)).astype(BF16)
            else:
                k_s[cb - GDN_QK_HEADS] = y.astype(BF16)
        else:
            v_s[cb - 2 * GDN_QK_HEADS] = y.astype(BF16)
    ext_ref[0:GDN_CARRY, :] = ext_ref[blk:blk + GDN_CARRY, :]

    row = c * blk + lax.broadcasted_iota(jnp.int32, (blk, 1), 0)
    valid = row >= LEAD_PAD
    e = e_ref[...]
    beta_all = jnp.where(valid, _sigmoid(e[:, :GDN_V_HEADS]), 0.0)
    g_all = jnp.where(valid, -jnp.exp(al_ref[...]) * _softplus(e[:, GDN_V_HEADS:] + dt_ref[...]), 0.0)

    for r0 in range(0, blk, GDN_CHUNK):
        rows = slice(r0, r0 + GDN_CHUNK)
        _gdn_chunk(r0, _cumsum_rows(g_all[rows]), beta_all[rows], s_ref, q_s, k_s, v_s, o_s, head_group)

    for jv in range(GDN_V_HEADS):
        sl = slice(jv * GDN_DV, (jv + 1) * GDN_DV)
        z = z_ref[:, sl].astype(F32)
        o_ref[:, sl] = (_rms_normalize(o_s[jv]) * nw_ref[...] * (z * _sigmoid(z))).astype(o_ref.dtype)


def _gdn_core(p_main, side, conv_w, a_log, dt_bias, norm_w, batch, rows_per_batch, *, head_group=32):
    blk = SEQ_BLOCK
    nblk = rows_per_batch // blk
    rows = p_main.shape[0]
    idx = lambda col: (lambda b, c: (_physical_block(b, c, nblk), col))
    const2 = lambda b, c: (0, 0)
    kern = functools.partial(_gdn_kernel, head_group=head_group)
    return pl.pallas_call(
        kern,
        grid=(batch, nblk),
        in_specs=[
            pl.BlockSpec((blk, GDN_CONV_CH), idx(0)),
            pl.BlockSpec((blk, GDN_V_W), idx(GDN_CONV_CH // GDN_V_W)),
            pl.BlockSpec((blk, 2 * GDN_V_HEADS), idx(0)),
            pl.BlockSpec((GDN_CONV, GDN_CONV_CH), const2),
            pl.BlockSpec((1, GDN_V_HEADS), const2),
            pl.BlockSpec((1, GDN_V_HEADS), const2),
            pl.BlockSpec((1, GDN_DV), const2),
        ],
        out_specs=pl.BlockSpec((blk, GDN_V_W), idx(0)),
        out_shape=jax.ShapeDtypeStruct((rows, GDN_V_W), BF16),
        scratch_shapes=[
            pltpu.VMEM((blk + GDN_CARRY, GDN_CONV_CH), F32),
            pltpu.VMEM((GDN_V_HEADS, GDN_DK, GDN_DV), F32),
            pltpu.VMEM((GDN_QK_HEADS, blk, GDN_DK), BF16),
            pltpu.VMEM((GDN_QK_HEADS, blk, GDN_DK), BF16),
            pltpu.VMEM((GDN_V_HEADS, blk, GDN_DV), BF16),
            pltpu.VMEM((GDN_V_HEADS, blk, GDN_DV), F32),
        ],
        compiler_params=_params("parallel", "arbitrary"),
        name="gdn_core",
    )(p_main, p_main, side, conv_w, a_log.reshape(1, -1), dt_bias.reshape(1, -1), norm_w.reshape(1, -1))


def _rope_table_kernel(cos_ref, sin_ref):
    shape = cos_ref.shape
    pos = (lax.broadcasted_iota(jnp.int32, shape, 0) - LEAD_PAD).astype(F32)
    pair = lax.rem(lax.broadcasted_iota(jnp.int32, shape, 1), SWA_DH // 2).astype(F32)
    inv = jnp.exp(pair * (-2.0 * math.log(ROPE_THETA) / SWA_DH))
    ang = pos * inv
    cos_ref[...] = jnp.cos(ang)
    sin_ref[...] = jnp.sin(ang)


def _rope_tables(rows_per_batch):
    shape = jax.ShapeDtypeStruct((rows_per_batch, LANES), F32)
    return pl.pallas_call(_rope_table_kernel, out_shape=[shape, shape], name="rope_tables")()


SWA_COL_BATCH = 4


def _swa_kernel(sink_ref, q_ref, kv_ref, kvp_ref, cos_ref, sin_ref, cosp_ref, sinp_ref, o_ref):
    n = pl.program_id(1)
    blk = SEQ_BLOCK
    lane = lax.broadcasted_iota(jnp.int32, (1, LANES), 1)
    low = lane < SWA_DH
    first_half = lax.rem(lane, SWA_DH) < SWA_DH // 2

    def rope(x, cos, sin):
        rot = jnp.where(first_half, -pltpu.roll(x, LANES - SWA_DH // 2, 1), pltpu.roll(x, SWA_DH // 2, 1))
        return x * cos + rot * sin

    cos, sin = cos_ref[...], sin_ref[...]
    cosp, sinp = cosp_ref[...], sinp_ref[...]

    stacked = lax.broadcasted_iota(jnp.int32, (2 * blk, 2 * blk), 0)
    q_row = n * blk + jnp.where(stacked < blk, stacked, stacked - blk)
    k_row = (n - 1) * blk + lax.broadcasted_iota(jnp.int32, (2 * blk, 2 * blk), 1)
    mask = (k_row <= q_row) & (q_row - k_row < SWA_WINDOW) & (k_row >= LEAD_PAD)
    top = lax.broadcasted_iota(jnp.int32, (2 * blk, 1), 0) < blk

    def both_heads(x, head_in_pair):
        swapped = pltpu.roll(x, SWA_DH, 1)
        return jnp.where(low, x, swapped) if head_in_pair == 0 else jnp.where(low, swapped, x)

    kk, v_stack = {}, {}
    for g in range(SWA_HKV):
        pair_col, head_in_pair = divmod(g, 2)
        ksl = slice(pair_col * LANES, (pair_col + 1) * LANES)
        vsl = slice(SWA_KV_W + pair_col * LANES, SWA_KV_W + (pair_col + 1) * LANES)
        k_cur = both_heads(rope(kv_ref[:, ksl].astype(F32), cos, sin), head_in_pair)
        k_prev = both_heads(rope(kvp_ref[:, ksl].astype(F32), cosp, sinp), head_in_pair)
        kk[g] = jnp.concatenate([k_prev, k_cur], axis=0).astype(BF16)
        v_cur = both_heads(kv_ref[:, vsl].astype(F32), head_in_pair)
        v_prev = both_heads(kvp_ref[:, vsl].astype(F32), head_in_pair)
        v2 = jnp.concatenate([v_prev, v_cur], axis=0)
        v_stack[g] = jnp.concatenate([jnp.where(low, v2, 0.0), jnp.where(low, 0.0, v2)], axis=0).astype(BF16)

    def scores(col):
        qp = rope(q_ref[:, col * LANES:(col + 1) * LANES].astype(F32), cos, sin)
        q2 = jnp.concatenate([jnp.where(low, qp, 0.0), jnp.where(low, 0.0, qp)], axis=0).astype(BF16)
        s = _dot_nt(q2, kk[col // (SWA_GROUP // 2)]) * (SWA_DH ** -0.5)
        return jnp.where(mask, s, -jnp.inf)

    def attend(cols, s):
        sink = {c: jnp.where(top, sink_ref[2 * c], sink_ref[2 * c + 1]) for c in cols}
        m = {c: jnp.maximum(jnp.max(s[c], axis=-1, keepdims=True), sink[c]) for c in cols}
        e = {c: jnp.exp(s[c] - m[c]) for c in cols}
        inv = {c: 1.0 / (jnp.sum(e[c], axis=-1, keepdims=True) + jnp.exp(sink[c] - m[c])) for c in cols}
        for c in cols:
            prob = e[c] * inv[c]
            p2 = jnp.concatenate([prob[:blk], prob[blk:]], axis=1).astype(BF16)
            o_ref[:, c * LANES:(c + 1) * LANES] = _dot(p2, v_stack[c // (SWA_GROUP // 2)]).astype(o_ref.dtype)

    n_cols = SWA_HQ // 2
    batches = [range(c0, c0 + SWA_COL_BATCH) for c0 in range(0, n_cols, SWA_COL_BATCH)]
    pending = {c: scores(c) for c in batches[0]}
    for bi, cols in enumerate(batches):
        current = {c: pending.pop(c) for c in cols}
        if bi + 1 < len(batches):
            pending.update({c: scores(c) for c in batches[bi + 1]})
        attend(cols, current)


def _swa_core(p_main, sinks, cos, sin, batch, rows_per_batch):
    blk = SEQ_BLOCK
    nblk = rows_per_batch // blk
    rows = p_main.shape[0]
    kv_col = SWA_Q_W // (2 * SWA_KV_W)
    cur = lambda col: (lambda b, n: (_physical_block(b, n, nblk), col))
    prev = lambda col: (lambda b, n: (_physical_block(b, jnp.maximum(n - 1, 0), nblk), col))
    tab_cur = lambda b, n: (n, 0)
    tab_prev = lambda b, n: (jnp.maximum(n - 1, 0), 0)
    return pl.pallas_call(
        _swa_kernel,
        grid=(batch, nblk),
        in_specs=[
            pl.BlockSpec(memory_space=pltpu.SMEM),
            pl.BlockSpec((blk, SWA_Q_W), cur(0)),
            pl.BlockSpec((blk, 2 * SWA_KV_W), cur(kv_col)),
            pl.BlockSpec((blk, 2 * SWA_KV_W), prev(kv_col)),
            pl.BlockSpec((blk, LANES), tab_cur),
            pl.BlockSpec((blk, LANES), tab_cur),
            pl.BlockSpec((blk, LANES), tab_prev),
            pl.BlockSpec((blk, LANES), tab_prev),
        ],
        out_specs=pl.BlockSpec((blk, SWA_Q_W), cur(0)),
        out_shape=jax.ShapeDtypeStruct((rows, SWA_Q_W), BF16),
        compiler_params=_params("parallel", "parallel"),
        name="swa_core",
    )(sinks, p_main, p_main, p_main, cos, sin, cos, sin)


def kernel(x, meta_tokens, norm_w, ffn_w_gate, ffn_w_up, ffn_w_down, mlstm_w_in, mlstm_b_if, mlstm_norm_w, mlstm_w_out, pool_w, pool_scale, gdn_w_in, gdn_conv_w, gdn_a_log, gdn_dt_bias, gdn_norm_w, gdn_w_out, swa_w_qkv, swa_b_qkv, swa_sinks, swa_w_out, swa_b_out, final_norm_w):
    batch, seq, d = x.shape
    rpb = _rows_per_batch(seq)
    zeros_d = jnp.zeros((d,), F32)
    h = x
    for i in range(DEPTH):
        m, j = i % N_MIXERS, i // N_MIXERS
        h = _ffn(h, norm_w[i, 0], ffn_w_gate, ffn_w_up, ffn_w_down, i, 0, rpb,
                 meta=meta_tokens.astype(x.dtype) if i == 0 else None)
        h = h.reshape(batch * rpb, d)
        if m == 0:
            p_main, gates = _norm_proj(h, norm_w[i, 1], mlstm_w_in[j], jnp.zeros((MLSTM_MAIN_W,), F32),
                                       MLSTM_MAIN_W, (mlstm_w_in[j][:, MLSTM_MAIN_W:], mlstm_b_if[j]))
            a = _mlstm_core(p_main, gates, mlstm_norm_w[j], batch, rpb)
            h = _proj_residual(a, mlstm_w_out[j], zeros_d, h, rpb)
        elif m == 1:
            h = _pool_mixer(h, norm_w[i, 1], pool_w[j], pool_scale[j], rpb)
        elif m == 2:
            p_main, side = _norm_proj(h, norm_w[i, 1], gdn_w_in[j], jnp.zeros((GDN_MAIN_W,), F32), GDN_MAIN_W,
                                      (gdn_w_in[j][:, GDN_MAIN_W:], jnp.zeros((2 * GDN_V_HEADS,), F32)))
            a = _gdn_core(p_main, side, gdn_conv_w[j], gdn_a_log[j], gdn_dt_bias[j], gdn_norm_w[j], batch, rpb)
            h = _proj_residual(a, gdn_w_out[j], zeros_d, h, rpb)
        else:
            (p_main,) = _norm_proj(h, norm_w[i, 1], swa_w_qkv[j], swa_b_qkv[j], SWA_IN)
            cos, sin = _rope_tables(rpb)
            a = _swa_core(p_main, swa_sinks[j], cos, sin, batch, rpb)
            h = _proj_residual(a, swa_w_out[j], swa_b_out[j], h, rpb)
        h = _ffn(h.reshape(batch, rpb, d), norm_w[i, 2], ffn_w_gate, ffn_w_up, ffn_w_down, i, 1, rpb,
                 final_w=final_norm_w if i == DEPTH - 1 else None)
    return h
```

```python
import functools
import math

import jax
import jax.numpy as jnp
from jax import lax
from jax.experimental import pallas as pl
from jax.experimental.pallas import tpu as pltpu

F32 = jnp.float32
BF16 = jnp.bfloat16

D_MODEL = 2048
DEPTH = 4
N_MIXERS = 4
N_META = 16
EPS = 1e-6

MLSTM_HEADS = 8
MLSTM_DV = 256
MLSTM_DK = 128
MLSTM_QK_W = MLSTM_HEADS * MLSTM_DK
MLSTM_V_W = MLSTM_HEADS * MLSTM_DV
MLSTM_MAIN_W = 2 * MLSTM_QK_W + 2 * MLSTM_V_W

POOL_WINDOWS = (2, 4, 8, 16)
POOL_GROUP = D_MODEL // len(POOL_WINDOWS)
POOL_HALO = 16

GDN_DK = 128
GDN_DV = 128
GDN_QK_HEADS = 16
GDN_V_HEADS = 32
GDN_CONV = 4
GDN_CHUNK = 64
GDN_QK_W = GDN_QK_HEADS * GDN_DK
GDN_V_W = GDN_V_HEADS * GDN_DV
GDN_CONV_CH = 2 * GDN_QK_W + GDN_V_W
GDN_MAIN_W = GDN_CONV_CH + GDN_V_W

SWA_DH = 64
SWA_HQ = 32
SWA_GROUP = 8
SWA_HKV = 4
SWA_WINDOW = 128
SWA_Q_W = SWA_HQ * SWA_DH
SWA_KV_W = SWA_HKV * SWA_DH
SWA_IN = SWA_Q_W + 2 * SWA_KV_W
ROPE_THETA = 10000.0

LANES = 128
SEQ_BLOCK = 128
LEAD_PAD = SEQ_BLOCK - N_META

VMEM_LIMIT_BYTES = 56 * 1024 * 1024
FFN_PROLOGUE_ROWS = 384


def _rows_per_batch(seq):
    assert seq % SEQ_BLOCK == 0, seq
    return seq + SEQ_BLOCK


def _divisor_tile(total, target, multiple):
    best = None
    for t in range(multiple, min(total, target) + 1, multiple):
        if total % t == 0:
            best = t
    assert best is not None, (total, target, multiple)
    return best


def _params(*semantics):
    return pltpu.CompilerParams(dimension_semantics=semantics, vmem_limit_bytes=VMEM_LIMIT_BYTES)


def _physical_block(b, n, nblk):
    return b * nblk + jnp.where(n == 0, nblk - 1, n - 1)


def _is_token_row(row_in_batch, seq):
    return (row_in_batch < seq) | (row_in_batch >= seq + LEAD_PAD)


def _rms_normalize(x):
    return x * lax.rsqrt(jnp.mean(x * x, axis=-1, keepdims=True) + EPS)


def _sigmoid(x):
    return 1.0 / (1.0 + jnp.exp(-x))


def _softplus(x):
    return jnp.maximum(x, 0.0) + jnp.log(1.0 + jnp.exp(-jnp.abs(x)))


def _dot(a, b):
    return jnp.dot(a, b, preferred_element_type=F32)


def _dot_nt(a, b):
    return lax.dot_general(a, b, (((1,), (1,)), ((), ())), preferred_element_type=F32)


def _dot_tn(a, b):
    return lax.dot_general(a, b, (((0,), (0,)), ((), ())), preferred_element_type=F32)


def _cumsum_rows(x):
    n = x.shape[0]
    row = lax.broadcasted_iota(jnp.int32, x.shape, 0)
    shift = 1
    while shift < n:
        x = x + jnp.where(row >= shift, pltpu.roll(x, shift, 0), 0.0)
        shift *= 2
    return x


def _column_to_row(col, eye):
    return jnp.sum(jnp.where(eye, col, 0.0), axis=0, keepdims=True)


def _ffn_kernel(*refs, assemble, final_norm, tiles_per_batch):
    h_ref, nw_ref, wg_ref, wu_ref, wd_ref = refs[:5]
    rest = list(refs[5:])
    meta_ref = rest.pop(0) if assemble else None
    fw_ref = rest.pop(0) if final_norm else None
    o_ref, xn_ref = rest
    j = pl.program_id(1)

    tm = o_ref.shape[0]
    sub = _divisor_tile(tm, FFN_PROLOGUE_ROWS, 16)
    assert sub >= SEQ_BLOCK

    def swiglu_rows(r0, r1, wg, wu, wd):
        xn = xn_ref[r0:r1, :]
        g = _dot(xn, wg)
        u = _dot(xn, wu)
        a = (0.5 * g * _sigmoid(g) * u).astype(BF16)
        o_ref[r0:r1, :] += _dot(a, wd)

    @pl.when(j == 0)
    def _():
        body_end = tm
        if assemble:
            body_end = tm - SEQ_BLOCK
            ends_batch = lax.rem(pl.program_id(0), tiles_per_batch) == tiles_per_batch - 1

            @pl.when(ends_batch)
            def _():
                o_ref[body_end:tm - N_META, :] = jnp.zeros((LEAD_PAD, o_ref.shape[1]), F32)
                o_ref[tm - N_META:, :] = meta_ref[...]

            @pl.when(jnp.logical_not(ends_batch))
            def _():
                o_ref[body_end:, :] = h_ref[body_end:, :]

        wg, wu, wd = (w[...].astype(BF16) for w in (wg_ref, wu_ref, wd_ref))
        for r0 in range(0, tm, sub):
            r1 = r0 + sub
            rc = min(r1, body_end)
            if rc > r0:
                o_ref[r0:rc, :] = h_ref[r0:rc, :]
            xn_ref[r0:r1, :] = (_rms_normalize(o_ref[r0:r1, :]) * nw_ref[...]).astype(BF16)
            swiglu_rows(r0, r1, wg, wu, wd)

    @pl.when(j > 0)
    def _():
        swiglu_rows(0, tm, *(w[...].astype(BF16) for w in (wg_ref, wu_ref, wd_ref)))

    if final_norm:
        @pl.when(j == pl.num_programs(1) - 1)
        def _():
            o_ref[...] = _rms_normalize(o_ref[...]) * fw_ref[...]


def _ffn(h, nw, w_gate, w_up, w_down, layer, half, rows_per_batch, *, meta=None, final_w=None,
         tm_target=1056, tf=256):
    batch, _, d = h.shape
    d_ff = w_gate.shape[-1]
    tm = _divisor_tile(rows_per_batch, tm_target, SEQ_BLOCK // 8)
    assert tm >= SEQ_BLOCK and d_ff % tf == 0
    tpb = rows_per_batch // tm
    row_spec = pl.BlockSpec((None, tm, d), lambda i, j: (i // tpb, i % tpb, 0))
    vec_spec = pl.BlockSpec((1, d), lambda i, j: (0, 0))
    in_specs = [
        row_spec, vec_spec,
        pl.BlockSpec((None, None, d, tf), lambda i, j: (layer, half, 0, j)),
        pl.BlockSpec((None, None, d, tf), lambda i, j: (layer, half, 0, j)),
        pl.BlockSpec((None, None, tf, d), lambda i, j: (layer, half, j, 0)),
    ]
    args = [h, nw.reshape(1, d), w_gate, w_up, w_down]
    if meta is not None:
        in_specs.append(pl.BlockSpec((N_META, d), lambda i, j: (0, 0)))
        args.append(meta)
    if final_w is not None:
        in_specs.append(vec_spec)
        args.append(final_w.reshape(1, d))
    out_rows = rows_per_batch - SEQ_BLOCK if final_w is not None else rows_per_batch
    kern = functools.partial(_ffn_kernel, assemble=meta is not None, final_norm=final_w is not None,
                             tiles_per_batch=tpb)
    return pl.pallas_call(
        kern,
        grid=(batch * tpb, d_ff // tf),
        in_specs=in_specs,
        out_specs=row_spec,
        out_shape=jax.ShapeDtypeStruct((batch, out_rows, d), F32),
        scratch_shapes=[pltpu.VMEM((tm, d), BF16)],
        compiler_params=_params("parallel", "arbitrary"),
        name="ffn",
    )(*args)


def _norm_proj_kernel(*refs, has_side, w_transposed):
    if has_side:
        h_ref, nw_ref, w_ref, b_ref, ws_ref, bs_ref, o_ref, os_ref, xn_ref = refs
    else:
        h_ref, nw_ref, w_ref, b_ref, o_ref, xn_ref = refs
    j = pl.program_id(1)
    tm = o_ref.shape[0]
    sub = _divisor_tile(tm, FFN_PROLOGUE_ROWS, 16)
    matmul = _dot_nt if w_transposed else _dot

    @pl.when(j == 0)
    def _():
        w = w_ref[...].astype(BF16)
        if has_side:
            n_side = os_ref.shape[1]
            ws = (ws_ref[:n_side, :] if w_transposed else ws_ref[:, :n_side]).astype(BF16)
        for r0 in range(0, tm, sub):
            rows = slice(r0, r0 + sub)
            xn = (_rms_normalize(h_ref[rows, :]) * nw_ref[...]).astype(BF16)
            xn_ref[rows, :] = xn
            o_ref[rows, :] = (matmul(xn, w) + b_ref[...]).astype(o_ref.dtype)
            if has_side:
                os_ref[rows, :] = matmul(xn, ws) + bs_ref[...]

    @pl.when(j > 0)
    def _():
        o_ref[...] = (matmul(xn_ref[...], w_ref[...].astype(BF16)) + b_ref[...]).astype(o_ref.dtype)


def _norm_proj(h, nw, w, bias, n_main, b_side=None, *, w_transposed=False, tm_target=1056, tn_target=1024):
    rows, d = h.shape
    n_total = w.shape[0] if w_transposed else w.shape[1]
    tm = _divisor_tile(rows, tm_target, 16)
    tn = _divisor_tile(n_main, tn_target, 2 * LANES)
    w_block = (lambda width, col: pl.BlockSpec((width, d), lambda i, j: (col(j), 0))) if w_transposed else \
              (lambda width, col: pl.BlockSpec((d, width), lambda i, j: (0, col(j))))
    in_specs = [
        pl.BlockSpec((tm, d), lambda i, j: (i, 0)),
        pl.BlockSpec((1, d), lambda i, j: (0, 0)),
        w_block(tn, lambda j: j),
        pl.BlockSpec((1, tn), lambda i, j: (0, j)),
    ]
    out_specs = [pl.BlockSpec((tm, tn), lambda i, j: (i, j))]
    out_shape = [jax.ShapeDtypeStruct((rows, n_main), BF16)]
    args = [h, nw.reshape(1, d), w, bias.reshape(1, -1)]
    if b_side is not None:
        n_side = n_total - n_main
        assert n_main % LANES == 0 and 0 < n_side <= LANES and b_side.shape == (n_side,)
        in_specs += [w_block(LANES, lambda j: n_main // LANES), pl.BlockSpec((1, n_side), lambda i, j: (0, 0))]
        out_specs += [pl.BlockSpec((tm, n_side), lambda i, j: (i, 0))]
        out_shape += [jax.ShapeDtypeStruct((rows, n_side), F32)]
        args += [w, b_side.reshape(1, n_side)]
    return pl.pallas_call(
        functools.partial(_norm_proj_kernel, has_side=b_side is not None, w_transposed=w_transposed),
        grid=(rows // tm, n_main // tn),
        in_specs=in_specs,
        out_specs=out_specs,
        out_shape=out_shape,
        scratch_shapes=[pltpu.VMEM((tm, d), BF16)],
        compiler_params=_params("parallel", "arbitrary"),
        name="norm_proj",
    )(*args)


def _proj_residual_kernel(a_ref, w_ref, b_ref, h_ref, o_ref, *, tm, rows_per_batch):
    i = pl.program_id(0)
    y = _dot(a_ref[...], w_ref[...].astype(BF16)) + b_ref[...]
    row = lax.rem(i * tm, rows_per_batch) + lax.broadcasted_iota(jnp.int32, (tm, 1), 0)
    o_ref[...] = h_ref[...] + jnp.where(_is_token_row(row, rows_per_batch - SEQ_BLOCK), y, 0.0)


def _proj_residual(a, w, bias, h, rows_per_batch, *, tm_target=1056, tn=512):
    rows, k = a.shape
    d = h.shape[1]
    tm = _divisor_tile(rows_per_batch, tm_target, 16)
    kern = functools.partial(_proj_residual_kernel, tm=tm, rows_per_batch=rows_per_batch)
    return pl.pallas_call(
        kern,
        grid=(rows // tm, d // tn),
        in_specs=[
            pl.BlockSpec((tm, k), lambda i, j: (i, 0)),
            pl.BlockSpec((k, tn), lambda i, j: (0, j)),
            pl.BlockSpec((1, tn), lambda i, j: (0, j)),
            pl.BlockSpec((tm, tn), lambda i, j: (i, j)),
        ],
        out_specs=pl.BlockSpec((tm, tn), lambda i, j: (i, j)),
        out_shape=jax.ShapeDtypeStruct((rows, d), F32),
        compiler_params=_params("parallel", "arbitrary"),
        name="proj_residual",
    )(a, w, bias.reshape(1, d), h)


def _mlstm_kernel(q_ref, k_ref, v_ref, og_ref, g_ref, nw_ref, o_ref, c_ref, n_ref, m_ref):
    c = pl.program_id(1)
    blk = SEQ_BLOCK
    heads = range(MLSTM_HEADS)

    @pl.when(c == 0)
    def _():
        c_ref[...] = jnp.zeros_like(c_ref)
        n_ref[...] = jnp.zeros_like(n_ref)
        m_ref[...] = jnp.zeros_like(m_ref)

    gates = g_ref[...]
    row = c * blk + lax.broadcasted_iota(jnp.int32, (blk, 1), 0)
    valid = row >= LEAD_PAD
    log_f = jnp.where(valid, jnp.minimum(gates, 0.0) - jnp.log(1.0 + jnp.exp(-jnp.abs(gates))), 0.0)
    b_all = _cumsum_rows(log_f)
    li_all = jnp.where(valid, gates, -jnp.inf)

    ri = lax.broadcasted_iota(jnp.int32, (blk, blk), 0)
    ci = lax.broadcasted_iota(jnp.int32, (blk, blk), 1)
    eye = ri == ci
    causal = ci <= ri
    scale = MLSTM_DK ** -0.5

    q = {hd: q_ref[:, hd * MLSTM_DK:(hd + 1) * MLSTM_DK] for hd in heads}
    k = {hd: k_ref[:, hd * MLSTM_DK:(hd + 1) * MLSTM_DK] for hd in heads}
    v = {hd: v_ref[:, hd * MLSTM_DV:(hd + 1) * MLSTM_DV] for hd in heads}
    c_st = {hd: c_ref[hd] for hd in heads}
    n_st = {hd: n_ref[hd] for hd in heads}
    m_prev = {hd: m_ref[hd][:, :1] for hd in heads}

    qk_raw = {hd: _dot_nt(q[hd], k[hd]) for hd in heads}
    q_c = {hd: _dot(q[hd], c_st[hd].astype(BF16)) for hd in heads}

    b_c = {hd: b_all[:, MLSTM_HEADS + hd:MLSTM_HEADS + hd + 1] for hd in heads}
    d_c = {hd: li_all[:, hd:hd + 1] - b_c[hd] for hd in heads}
    b_last = {hd: b_c[hd][blk - 1:blk, :] for hd in heads}

    a_init, ka = {}, {}
    for hd in heads:
        log_end_init = b_last[hd] + m_prev[hd]
        log_end = b_last[hd] + d_c[hd]
        m_new = jnp.maximum(log_end_init, jnp.max(log_end, axis=0, keepdims=True))
        a_init[hd] = jnp.exp(log_end_init - m_new)
        ka[hd] = k[hd].astype(F32) * jnp.exp(log_end - m_new)
        m_ref[hd] = jnp.broadcast_to(m_new, (1, LANES))
    kv = {hd: _dot_tn(ka[hd].astype(BF16), v[hd]) for hd in heads}

    d_r = {hd: _column_to_row(d_c[hd], eye) for hd in heads}
    log_w = {hd: jnp.where(causal, b_c[hd] + d_r[hd], -jnp.inf) for hd in heads}
    log_init = {hd: b_c[hd] + m_prev[hd] for hd in heads}
    m_t = {hd: jnp.maximum(log_init[hd], jnp.max(log_w[hd], axis=-1, keepdims=True)) for hd in heads}
    w_init = {hd: jnp.exp(log_init[hd] - m_t[hd]) * scale for hd in heads}
    qk = {hd: qk_raw[hd] * (jnp.exp(log_w[hd] - m_t[hd]) * scale) for hd in heads}
    pv = {hd: _dot(qk[hd].astype(BF16), v[hd]) for hd in heads}
    qn = {hd: jnp.sum(q[hd].astype(F32) * n_st[hd], axis=-1, keepdims=True) for hd in heads}
    den = {hd: w_init[hd] * qn[hd] + jnp.sum(qk[hd], axis=-1, keepdims=True) for hd in heads}
    for hd in heads:
        hh = (w_init[hd] * q_c[hd] + pv[hd]) / jnp.maximum(jnp.abs(den[hd]), jnp.exp(-m_t[hd]))
        sl = slice(hd * MLSTM_DV, (hd + 1) * MLSTM_DV)
        gate = _sigmoid(og_ref[:, sl].astype(F32))
        o_ref[:, sl] = (_rms_normalize(hh) * nw_ref[:, sl] * gate).astype(o_ref.dtype)

    for hd in heads:
        c_ref[hd] = a_init[hd] * c_st[hd] + kv[hd]
        n_ref[hd] = a_init[hd] * n_st[hd] + jnp.sum(ka[hd], axis=0, keepdims=True)


def _mlstm_core(p_main, gates, norm_w, batch, rows_per_batch):
    blk = SEQ_BLOCK
    nblk = rows_per_batch // blk
    rows = p_main.shape[0]
    idx = lambda col: (lambda b, c: (_physical_block(b, c, nblk), col))
    return pl.pallas_call(
        _mlstm_kernel,
        grid=(batch, nblk),
        in_specs=[
            pl.BlockSpec((blk, MLSTM_QK_W), idx(0)),
            pl.BlockSpec((blk, MLSTM_QK_W), idx(1)),
            pl.BlockSpec((blk, MLSTM_V_W), idx(1)),
            pl.BlockSpec((blk, MLSTM_V_W), idx(2)),
            pl.BlockSpec((blk, 2 * MLSTM_HEADS), idx(0)),
            pl.BlockSpec((1, MLSTM_V_W), lambda b, c: (0, 0)),
        ],
        out_specs=pl.BlockSpec((blk, MLSTM_V_W), idx(0)),
        out_shape=jax.ShapeDtypeStruct((rows, MLSTM_V_W), BF16),
        scratch_shapes=[
            pltpu.VMEM((MLSTM_HEADS, MLSTM_DK, MLSTM_DV), F32),
            pltpu.VMEM((MLSTM_HEADS, 1, MLSTM_DK), F32),
            pltpu.VMEM((MLSTM_HEADS, 1, LANES), F32),
        ],
        compiler_params=_params("parallel", "arbitrary"),
        name="mlstm_core",
    )(p_main, p_main, p_main, p_main, gates, norm_w.reshape(1, MLSTM_V_W))


def _pool_kernel(h_ref, halo_ref, nw_ref, w_ref, sc_ref, o_ref, ext_ref, *, tm, rows_per_batch):
    i = pl.program_id(0)
    seq = rows_per_batch - SEQ_BLOCK
    x = h_ref[...]
    nw = nw_ref[...]
    u = _rms_normalize(x) * nw
    ext_ref[0:POOL_HALO, :] = _rms_normalize(halo_ref[...]) * nw
    ext_ref[POOL_HALO:, :] = u
    row = lax.rem(i * tm, rows_per_batch) + lax.broadcasted_iota(jnp.int32, (tm, 1), 0)
    pos = jnp.where(row < seq, row + N_META, row - (seq + LEAD_PAD))
    valid = pos >= 0
    for gi, win in enumerate(POOL_WINDOWS):
        sl = slice(gi * POOL_GROUP, (gi + 1) * POOL_GROUP)
        terms = [ext_ref[POOL_HALO - s:POOL_HALO - s + tm, sl] for s in range(win)]
        while len(terms) > 1:
            terms = [terms[a] + terms[a + 1] for a in range(0, len(terms), 2)]
        count = jnp.clip(pos + 1, 1, win).astype(F32)
        pooled = terms[0] / count - u[:, sl]
        y = _dot(pooled.astype(BF16), w_ref[gi].astype(BF16)) * sc_ref[:, sl]
        o_ref[:, sl] = x[:, sl] + jnp.where(valid, y, 0.0)


def _pool_mixer(h, nw, w_group, scale, rows_per_batch, *, tm_target=384):
    rows, d = h.shape
    tm = _divisor_tile(rows_per_batch, tm_target, POOL_HALO)
    tpb = rows_per_batch // tm
    halo_per_tile = tm // POOL_HALO
    halo_per_batch = rows_per_batch // POOL_HALO

    def halo_index(i):
        return jnp.where(i % tpb == 0, (i // tpb + 1) * halo_per_batch - 1, i * halo_per_tile - 1), 0

    kern = functools.partial(_pool_kernel, tm=tm, rows_per_batch=rows_per_batch)
    return pl.pallas_call(
        kern,
        grid=(rows // tm,),
        in_specs=[
            pl.BlockSpec((tm, d), lambda i: (i, 0)),
            pl.BlockSpec((POOL_HALO, d), halo_index),
            pl.BlockSpec((1, d), lambda i: (0, 0)),
            pl.BlockSpec(w_group.shape, lambda i: (0, 0, 0)),
            pl.BlockSpec((1, d), lambda i: (0, 0)),
        ],
        out_specs=pl.BlockSpec((tm, d), lambda i: (i, 0)),
        out_shape=jax.ShapeDtypeStruct((rows, d), F32),
        scratch_shapes=[pltpu.VMEM((tm + POOL_HALO, d), F32)],
        compiler_params=_params("parallel"),
        name="pool_mixer",
    )(h, h, nw.reshape(1, d), w_group, scale.reshape(1, d))


GDN_CARRY = 8


def _unit_lower_inverses(strict_lowers):
    n = strict_lowers[0].shape[0]
    ri = lax.broadcasted_iota(jnp.int32, (n, n), 0)
    ci = lax.broadcasted_iota(jnp.int32, (n, n), 1)
    ident = jnp.where(ri == ci, 1.0, 0.0)
    invs = [ident - l for l in strict_lowers]
    powers_b = [l.astype(BF16) for l in strict_lowers]
    terms = 2
    while terms < n:
        powers_b = [_dot(ab, ab).astype(BF16) for ab in powers_b]
        invs = [p + _dot(p.astype(BF16), ab) for p, ab in zip(invs, powers_b)]
        terms *= 2
    return invs


def _gdn_chunk(r0, gc_all, beta_all, s_ref, q_s, k_s, v_s, o_s, head_group):
    ch = GDN_CHUNK
    rows = slice(r0, r0 + ch)
    ri = lax.broadcasted_iota(jnp.int32, (ch, ch), 0)
    ci = lax.broadcasted_iota(jnp.int32, (ch, ch), 1)
    eye = ri == ci
    causal = ci <= ri
    strict = ci < ri
    rep = GDN_V_HEADS // GDN_QK_HEADS
    for g0 in range(0, GDN_V_HEADS, head_group):
        vheads = range(g0, g0 + head_group)
        qheads = range(g0 // rep, (g0 + head_group) // rep)
        q = {j: q_s[j, rows, :] for j in qheads}
        k = {j: k_s[j, rows, :] for j in qheads}
        kk = {j: _dot_nt(k[j], k[j]) for j in qheads}
        qk = {j: _dot_nt(q[j], k[j]) for j in qheads}
        gc_c = {jv: gc_all[:, jv:jv + 1] for jv in vheads}
        beta_c = {jv: beta_all[:, jv:jv + 1] for jv in vheads}
        decay = {jv: jnp.exp(jnp.where(causal, gc_c[jv] - _column_to_row(gc_c[jv], eye), -jnp.inf)) for jv in vheads}
        t_inv = _unit_lower_inverses([jnp.where(strict, kk[jv // rep] * decay[jv], 0.0) * beta_c[jv] for jv in vheads])
        t_inv = dict(zip(vheads, t_inv))
        egc = {jv: jnp.exp(gc_c[jv]) for jv in vheads}
        s_st = {jv: s_ref[jv] for jv in vheads}
        s_b = {jv: s_st[jv].astype(BF16) for jv in vheads}
        qs = {jv: _dot(q[jv // rep], s_b[jv]) * egc[jv] for jv in vheads}
        rhs = {jv: jnp.concatenate([v_s[jv, rows, :].astype(F32) * beta_c[jv],
                                    k[jv // rep].astype(F32) * (beta_c[jv] * egc[jv])], axis=-1).astype(BF16)
               for jv in vheads}
        sol = {jv: _dot(t_inv[jv].astype(BF16), rhs[jv]) for jv in vheads}
        v_new = {jv: sol[jv][:, :GDN_DV] - _dot(sol[jv][:, GDN_DV:].astype(BF16), s_b[jv]) for jv in vheads}
        for jv in vheads:
            o_s[jv, rows, :] = qs[jv] + _dot((qk[jv // rep] * decay[jv]).astype(BF16), v_new[jv].astype(BF16))
        for jv in vheads:
            g_last = gc_c[jv][ch - 1:ch, :]
            v_dec = (v_new[jv] * jnp.exp(g_last - gc_c[jv])).astype(BF16)
            s_ref[jv] = jnp.exp(g_last) * s_st[jv] + _dot_tn(k[jv // rep], v_dec)


def _gdn_kernel(x_ref, z_ref, e_ref, cw_ref, al_ref, dt_ref, nw_ref, o_ref,
                ext_ref, s_ref, q_s, k_s, v_s, o_s, *, head_group):
    c = pl.program_id(1)
    blk = SEQ_BLOCK

    @pl.when(c == 0)
    def _():
        ext_ref[0:GDN_CARRY, :] = jnp.zeros((GDN_CARRY, GDN_CONV_CH), F32)
        s_ref[...] = jnp.zeros_like(s_ref)

    ext_ref[GDN_CARRY:, :] = x_ref[...].astype(F32)
    n_cols = GDN_CONV_CH // LANES
    for cb in range(n_cols):
        sl = slice(cb * LANES, (cb + 1) * LANES)
        acc = None
        for tap in range(GDN_CONV):
            off = GDN_CARRY - (GDN_CONV - 1) + tap
            term = ext_ref[off:off + blk, sl] * cw_ref[tap:tap + 1, sl]
            acc = term if acc is None else acc + term
        y = acc * _sigmoid(acc)
        if cb < 2 * GDN_QK_HEADS:
            y = y * lax.rsqrt(jnp.sum(y * y, axis=-1, keepdims=True) + EPS)
            if cb < GDN_QK_HEADS:
                q_s[cb] = (y * (GDN_DK ** -0.5)).astype(BF16)
            else:
                k_s[cb - GDN_QK_HEADS] = y.astype(BF16)
        else:
            v_s[cb - 2 * GDN_QK_HEADS] = y.astype(BF16)
    ext_ref[0:GDN_CARRY, :] = ext_ref[blk:blk + GDN_CARRY, :]

    row = c * blk + lax.broadcasted_iota(jnp.int32, (blk, 1), 0)
    valid = row >= LEAD_PAD
    e = e_ref[...]
    beta_all = jnp.where(valid, _sigmoid(e[:, :GDN_V_HEADS]), 0.0)
    g_all = jnp.where(valid, -jnp.exp(al_ref[...]) * _softplus(e[:, GDN_V_HEADS:] + dt_ref[...]), 0.0)

    for r0 in range(0, blk, GDN_CHUNK):
        rows = slice(r0, r0 + GDN_CHUNK)
        _gdn_chunk(r0, _cumsum_rows(g_all[rows]), beta_all[rows], s_ref, q_s, k_s, v_s, o_s, head_group)

    for jv in range(GDN_V_HEADS):
        sl = slice(jv * GDN_DV, (jv + 1) * GDN_DV)
        z = z_ref[:, sl].astype(F32)
        o_ref[:, sl] = (_rms_normalize(o_s[jv]) * nw_ref[...] * (z * _sigmoid(z))).astype(o_ref.dtype)


def _gdn_core(p_main, side, conv_w, a_log, dt_bias, norm_w, batch, rows_per_batch, *, head_group=32):
    blk = SEQ_BLOCK
    nblk = rows_per_batch // blk
    rows = p_main.shape[0]
    idx = lambda col: (lambda b, c: (_physical_block(b, c, nblk), col))
    const2 = lambda b, c: (0, 0)
    kern = functools.partial(_gdn_kernel, head_group=head_group)
    return pl.pallas_call(
        kern,
        grid=(batch, nblk),
        in_specs=[
            pl.BlockSpec((blk, GDN_CONV_CH), idx(0)),
            pl.BlockSpec((blk, GDN_V_W), idx(GDN_CONV_CH // GDN_V_W)),
            pl.BlockSpec((blk, 2 * GDN_V_HEADS), idx(0)),
            pl.BlockSpec((GDN_CONV, GDN_CONV_CH), const2),
            pl.BlockSpec((1, GDN_V_HEADS), const2),
            pl.BlockSpec((1, GDN_V_HEADS), const2),
            pl.BlockSpec((1, GDN_DV), const2),
        ],
        out_specs=pl.BlockSpec((blk, GDN_V_W), idx(0)),
        out_shape=jax.ShapeDtypeStruct((rows, GDN_V_W), BF16),
        scratch_shapes=[
            pltpu.VMEM((blk + GDN_CARRY, GDN_CONV_CH), F32),
            pltpu.VMEM((GDN_V_HEADS, GDN_DK, GDN_DV), F32),
            pltpu.VMEM((GDN_QK_HEADS, blk, GDN_DK), BF16),
            pltpu.VMEM((GDN_QK_HEADS, blk, GDN_DK), BF16),
            pltpu.VMEM((GDN_V_HEADS, blk, GDN_DV), BF16),
            pltpu.VMEM((GDN_V_HEADS, blk, GDN_DV), F32),
        ],
        compiler_params=_params("parallel", "arbitrary"),
        name="gdn_core",
    )(p_main, p_main, side, conv_w, a_log.reshape(1, -1), dt_bias.reshape(1, -1), norm_w.reshape(1, -1))


def _rope_table_kernel(cos_ref, sin_ref):
    shape = cos_ref.shape
    pos = (lax.broadcasted_iota(jnp.int32, shape, 0) - LEAD_PAD).astype(F32)
    pair = lax.rem(lax.broadcasted_iota(jnp.int32, shape, 1), SWA_DH // 2).astype(F32)
    inv = jnp.exp(pair * (-2.0 * math.log(ROPE_THETA) / SWA_DH))
    ang = pos * inv
    cos_ref[...] = jnp.cos(ang)
    sin_ref[...] = jnp.sin(ang)


def _rope_tables(rows_per_batch):
    shape = jax.ShapeDtypeStruct((rows_per_batch, LANES), F32)
    return pl.pallas_call(_rope_table_kernel, out_shape=[shape, shape], name="rope_tables")()


SWA_COL_BATCH = 4


def _swa_kernel(sink_ref, q_ref, kv_ref, kvp_ref, cos_ref, sin_ref, cosp_ref, sinp_ref, o_ref):
    n = pl.program_id(1)
    blk = SEQ_BLOCK
    lane = lax.broadcasted_iota(jnp.int32, (1, LANES), 1)
    low = lane < SWA_DH
    first_half = lax.rem(lane, SWA_DH) < SWA_DH // 2

    def rope(x, cos, sin):
        rot = jnp.where(first_half, -pltpu.roll(x, LANES - SWA_DH // 2, 1), pltpu.roll(x, SWA_DH // 2, 1))
        return x * cos + rot * sin

    cos, sin = cos_ref[...], sin_ref[...]
    cosp, sinp = cosp_ref[...], sinp_ref[...]

    stacked = lax.broadcasted_iota(jnp.int32, (2 * blk, 2 * blk), 0)
    q_row = n * blk + jnp.where(stacked < blk, stacked, stacked - blk)
    k_row = (n - 1) * blk + lax.broadcasted_iota(jnp.int32, (2 * blk, 2 * blk), 1)
    mask = (k_row <= q_row) & (q_row - k_row < SWA_WINDOW) & (k_row >= LEAD_PAD)
    top = lax.broadcasted_iota(jnp.int32, (2 * blk, 1), 0) < blk

    def both_heads(x, head_in_pair):
        swapped = pltpu.roll(x, SWA_DH, 1)
        return jnp.where(low, x, swapped) if head_in_pair == 0 else jnp.where(low, swapped, x)

    kk, v_stack = {}, {}
    for g in range(SWA_HKV):
        pair_col, head_in_pair = divmod(g, 2)
        ksl = slice(pair_col * LANES, (pair_col + 1) * LANES)
        vsl = slice(SWA_KV_W + pair_col * LANES, SWA_KV_W + (pair_col + 1) * LANES)
        k_cur = both_heads(rope(kv_ref[:, ksl].astype(F32), cos, sin), head_in_pair)
        k_prev = both_heads(rope(kvp_ref[:, ksl].astype(F32), cosp, sinp), head_in_pair)
        kk[g] = jnp.concatenate([k_prev, k_cur], axis=0).astype(BF16)
        v_cur = both_heads(kv_ref[:, vsl].astype(F32), head_in_pair)
        v_prev = both_heads(kvp_ref[:, vsl].astype(F32), head_in_pair)
        v2 = jnp.concatenate([v_prev, v_cur], axis=0)
        v_stack[g] = jnp.concatenate([jnp.where(low, v2, 0.0), jnp.where(low, 0.0, v2)], axis=0).astype(BF16)

    def scores(col):
        qp = rope(q_ref[:, col * LANES:(col + 1) * LANES].astype(F32), cos, sin)
        q2 = jnp.concatenate([jnp.where(low, qp, 0.0), jnp.where(low, 0.0, qp)], axis=0).astype(BF16)
        s = _dot_nt(q2, kk[col // (SWA_GROUP // 2)]) * (SWA_DH ** -0.5)
        return jnp.where(mask, s, -jnp.inf)

    def attend(cols, s):
        sink = {c: jnp.where(top, sink_ref[2 * c], sink_ref[2 * c + 1]) for c in cols}
        m = {c: jnp.maximum(jnp.max(s[c], axis=-1, keepdims=True), sink[c]) for c in cols}
        e = {c: jnp.exp(s[c] - m[c]) for c in cols}
        inv = {c: 1.0 / (jnp.sum(e[c], axis=-1, keepdims=True) + jnp.exp(sink[c] - m[c])) for c in cols}
        for c in cols:
            prob = e[c] * inv[c]
            p2 = jnp.concatenate([prob[:blk], prob[blk:]], axis=1).astype(BF16)
            o_ref[:, c * LANES:(c + 1) * LANES] = _dot(p2, v_stack[c // (SWA_GROUP // 2)]).astype(o_ref.dtype)

    n_cols = SWA_HQ // 2
    batches = [range(c0, c0 + SWA_COL_BATCH) for c0 in range(0, n_cols, SWA_COL_BATCH)]
    pending = {c: scores(c) for c in batches[0]}
    for bi, cols in enumerate(batches):
        current = {c: pending.pop(c) for c in cols}
        if bi + 1 < len(batches):
            pending.update({c: scores(c) for c in batches[bi + 1]})
        attend(cols, current)


def _swa_core(p_main, sinks, cos, sin, batch, rows_per_batch):
    blk = SEQ_BLOCK
    nblk = rows_per_batch // blk
    rows = p_main.shape[0]
    kv_col = SWA_Q_W // (2 * SWA_KV_W)
    cur = lambda col: (lambda b, n: (_physical_block(b, n, nblk), col))
    prev = lambda col: (lambda b, n: (_physical_block(b, jnp.maximum(n - 1, 0), nblk), col))
    tab_cur = lambda b, n: (n, 0)
    tab_prev = lambda b, n: (jnp.maximum(n - 1, 0), 0)
    return pl.pallas_call(
        _swa_kernel,
        grid=(batch, nblk),
        in_specs=[
            pl.BlockSpec(memory_space=pltpu.SMEM),
            pl.BlockSpec((blk, SWA_Q_W), cur(0)),
            pl.BlockSpec((blk, 2 * SWA_KV_W), cur(kv_col)),
            pl.BlockSpec((blk, 2 * SWA_KV_W), prev(kv_col)),
            pl.BlockSpec((blk, LANES), tab_cur),
            pl.BlockSpec((blk, LANES), tab_cur),
            pl.BlockSpec((blk, LANES), tab_prev),
            pl.BlockSpec((blk, LANES), tab_prev),
        ],
        out_specs=pl.BlockSpec((blk, SWA_Q_W), cur(0)),
        out_shape=jax.ShapeDtypeStruct((rows, SWA_Q_W), BF16),
        compiler_params=_params("parallel", "parallel"),
        name="swa_core",
    )(sinks, p_main, p_main, p_main, cos, sin, cos, sin)


def kernel(x, meta_tokens, norm_w, ffn_w_gate, ffn_w_up, ffn_w_down, mlstm_w_in, mlstm_b_if, mlstm_norm_w, mlstm_w_out, pool_w, pool_scale, gdn_w_in, gdn_conv_w, gdn_a_log, gdn_dt_bias, gdn_norm_w, gdn_w_out, swa_w_qkv, swa_b_qkv, swa_sinks, swa_w_out, swa_b_out, final_norm_w):
    batch, seq, d = x.shape
    rpb = _rows_per_batch(seq)
    zeros_d = jnp.zeros((d,), F32)
    h = x
    for i in range(DEPTH):
        m, j = i % N_MIXERS, i // N_MIXERS
        h = _ffn(h, norm_w[i, 0], ffn_w_gate, ffn_w_up, ffn_w_down, i, 0, rpb,
                 meta=meta_tokens.astype(x.dtype) if i == 0 else None)
        h = h.reshape(batch * rpb, d)
        if m == 0:
            p_main, gates = _norm_proj(h, norm_w[i, 1], mlstm_w_in[j].T, jnp.zeros((MLSTM_MAIN_W,), F32),
                                       MLSTM_MAIN_W, mlstm_b_if[j], w_transposed=True)
            a = _mlstm_core(p_main, gates, mlstm_norm_w[j], batch, rpb)
            h = _proj_residual(a, mlstm_w_out[j], zeros_d, h, rpb)
        elif m == 1:
            h = _pool_mixer(h, norm_w[i, 1], pool_w[j], pool_scale[j], rpb)
        elif m == 2:
            p_main, side = _norm_proj(h, norm_w[i, 1], gdn_w_in[j].T, jnp.zeros((GDN_MAIN_W,), F32), GDN_MAIN_W,
                                      jnp.zeros((2 * GDN_V_HEADS,), F32), w_transposed=True)
            a = _gdn_core(p_main, side, gdn_conv_w[j], gdn_a_log[j], gdn_dt_bias[j], gdn_norm_w[j], batch, rpb)
            h = _proj_residual(a, gdn_w_out[j], zeros_d, h, rpb)
        else:
            (p_main,) = _norm_proj(h, norm_w[i, 1], swa_w_qkv[j], swa_b_qkv[j], SWA_IN)
            cos, sin = _rope_tables(rpb)
            a = _swa_core(p_main, swa_sinks[j], cos, sin, batch, rpb)
            h = _proj_residual(a, swa_w_out[j], swa_b_out[j], h, rpb)
        h = _ffn(h.reshape(batch, rpb, d), norm_w[i, 2], ffn_w_gate, ffn_w_up, ffn_w_down, i, 1, rpb,
                 final_w=final_norm_w if i == DEPTH - 1 else None)
    return h
```

```python
import functools
import math

import jax
import jax.numpy as jnp
from jax import lax
from jax.experimental import pallas as pl
from jax.experimental.pallas import tpu as pltpu

F32 = jnp.float32
BF16 = jnp.bfloat16

D_MODEL = 2048
DEPTH = 4
N_MIXERS = 4
N_META = 16
EPS = 1e-6

MLSTM_HEADS = 8
MLSTM_DV = 256
MLSTM_DK = 128
MLSTM_QK_W = MLSTM_HEADS * MLSTM_DK
MLSTM_V_W = MLSTM_HEADS * MLSTM_DV
MLSTM_MAIN_W = 2 * MLSTM_QK_W + 2 * MLSTM_V_W

POOL_WINDOWS = (2, 4, 8, 16)
POOL_GROUP = D_MODEL // len(POOL_WINDOWS)
POOL_HALO = 16

GDN_DK = 128
GDN_DV = 128
GDN_QK_HEADS = 16
GDN_V_HEADS = 32
GDN_CONV = 4
GDN_CHUNK = 64
GDN_QK_W = GDN_QK_HEADS * GDN_DK
GDN_V_W = GDN_V_HEADS * GDN_DV
GDN_CONV_CH = 2 * GDN_QK_W + GDN_V_W
GDN_MAIN_W = GDN_CONV_CH + GDN_V_W

SWA_DH = 64
SWA_HQ = 32
SWA_GROUP = 8
SWA_HKV = 4
SWA_WINDOW = 128
SWA_Q_W = SWA_HQ * SWA_DH
SWA_KV_W = SWA_HKV * SWA_DH
SWA_IN = SWA_Q_W + 2 * SWA_KV_W
ROPE_THETA = 10000.0

LANES = 128
SEQ_BLOCK = 128
LEAD_PAD = SEQ_BLOCK - N_META

VMEM_LIMIT_BYTES = 56 * 1024 * 1024
FFN_PROLOGUE_ROWS = 384


def _rows_per_batch(seq):
    assert seq % SEQ_BLOCK == 0, seq
    return seq + SEQ_BLOCK


def _divisor_tile(total, target, multiple):
    best = None
    for t in range(multiple, min(total, target) + 1, multiple):
        if total % t == 0:
            best = t
    assert best is not None, (total, target, multiple)
    return best


def _params(*semantics):
    return pltpu.CompilerParams(dimension_semantics=semantics, vmem_limit_bytes=VMEM_LIMIT_BYTES)


def _physical_block(b, n, nblk):
    return b * nblk + jnp.where(n == 0, nblk - 1, n - 1)


def _is_token_row(row_in_batch, seq):
    return (row_in_batch < seq) | (row_in_batch >= seq + LEAD_PAD)


def _rms_normalize(x):
    return x * lax.rsqrt(jnp.mean(x * x, axis=-1, keepdims=True) + EPS)


def _sigmoid(x):
    return 1.0 / (1.0 + jnp.exp(-x))


def _softplus(x):
    return jnp.maximum(x, 0.0) + jnp.log(1.0 + jnp.exp(-jnp.abs(x)))


def _dot(a, b):
    return jnp.dot(a, b, preferred_element_type=F32)


def _dot_nt(a, b):
    return lax.dot_general(a, b, (((1,), (1,)), ((), ())), preferred_element_type=F32)


def _dot_tn(a, b):
    return lax.dot_general(a, b, (((0,), (0,)), ((), ())), preferred_element_type=F32)


def _cumsum_rows(x):
    n = x.shape[0]
    row = lax.broadcasted_iota(jnp.int32, x.shape, 0)
    shift = 1
    while shift < n:
        x = x + jnp.where(row >= shift, pltpu.roll(x, shift, 0), 0.0)
        shift *= 2
    return x


def _column_to_row(col, eye):
    return jnp.sum(jnp.where(eye, col, 0.0), axis=0, keepdims=True)


def _ffn_kernel(*refs, assemble, final_norm, tiles_per_batch):
    h_ref, nw_ref, wg_ref, wu_ref, wd_ref = refs[:5]
    rest = list(refs[5:])
    meta_ref = rest.pop(0) if assemble else None
    fw_ref = rest.pop(0) if final_norm else None
    o_ref, xn_ref = rest
    j = pl.program_id(1)

    tm = o_ref.shape[0]
    sub = _divisor_tile(tm, FFN_PROLOGUE_ROWS, 16)
    assert sub >= SEQ_BLOCK

    def swiglu_rows(r0, r1, wg, wu, wd):
        xn = xn_ref[r0:r1, :]
        g = _dot(xn, wg)
        u = _dot(xn, wu)
        a = (0.5 * g * _sigmoid(g) * u).astype(BF16)
        o_ref[r0:r1, :] += _dot(a, wd)

    @pl.when(j == 0)
    def _():
        body_end = tm
        if assemble:
            body_end = tm - SEQ_BLOCK
            ends_batch = lax.rem(pl.program_id(0), tiles_per_batch) == tiles_per_batch - 1

            @pl.when(ends_batch)
            def _():
                o_ref[body_end:tm - N_META, :] = jnp.zeros((LEAD_PAD, o_ref.shape[1]), F32)
                o_ref[tm - N_META:, :] = meta_ref[...]

            @pl.when(jnp.logical_not(ends_batch))
            def _():
                o_ref[body_end:, :] = h_ref[body_end:, :]

        wg, wu, wd = (w[...].astype(BF16) for w in (wg_ref, wu_ref, wd_ref))
        for r0 in range(0, tm, sub):
            r1 = r0 + sub
            rc = min(r1, body_end)
            if rc > r0:
                o_ref[r0:rc, :] = h_ref[r0:rc, :]
            xn_ref[r0:r1, :] = (_rms_normalize(o_ref[r0:r1, :]) * nw_ref[...]).astype(BF16)
            swiglu_rows(r0, r1, wg, wu, wd)

    @pl.when(j > 0)
    def _():
        swiglu_rows(0, tm, *(w[...].astype(BF16) for w in (wg_ref, wu_ref, wd_ref)))

    if final_norm:
        @pl.when(j == pl.num_programs(1) - 1)
        def _():
            o_ref[...] = _rms_normalize(o_ref[...]) * fw_ref[...]


def _ffn(h, nw, w_gate, w_up, w_down, layer, half, rows_per_batch, *, meta=None, final_w=None,
         tm_target=1056, tf=256):
    batch, _, d = h.shape
    d_ff = w_gate.shape[-1]
    tm = _divisor_tile(rows_per_batch, tm_target, SEQ_BLOCK // 8)
    assert tm >= SEQ_BLOCK and d_ff % tf == 0
    tpb = rows_per_batch // tm
    row_spec = pl.BlockSpec((None, tm, d), lambda i, j: (i // tpb, i % tpb, 0))
    vec_spec = pl.BlockSpec((1, d), lambda i, j: (0, 0))
    in_specs = [
        row_spec, vec_spec,
        pl.BlockSpec((None, None, d, tf), lambda i, j: (layer, half, 0, j)),
        pl.BlockSpec((None, None, d, tf), lambda i, j: (layer, half, 0, j)),
        pl.BlockSpec((None, None, tf, d), lambda i, j: (layer, half, j, 0)),
    ]
    args = [h, nw.reshape(1, d), w_gate, w_up, w_down]
    if meta is not None:
        in_specs.append(pl.BlockSpec((N_META, d), lambda i, j: (0, 0)))
        args.append(meta)
    if final_w is not None:
        in_specs.append(vec_spec)
        args.append(final_w.reshape(1, d))
    out_rows = rows_per_batch - SEQ_BLOCK if final_w is not None else rows_per_batch
    kern = functools.partial(_ffn_kernel, assemble=meta is not None, final_norm=final_w is not None,
                             tiles_per_batch=tpb)
    return pl.pallas_call(
        kern,
        grid=(batch * tpb, d_ff // tf),
        in_specs=in_specs,
        out_specs=row_spec,
        out_shape=jax.ShapeDtypeStruct((batch, out_rows, d), F32),
        scratch_shapes=[pltpu.VMEM((tm, d), BF16)],
        compiler_params=_params("parallel", "arbitrary"),
        name="ffn",
    )(*args)


def _norm_proj_kernel(*refs, has_side, w_transposed):
    if has_side:
        h_ref, nw_ref, w_ref, b_ref, ws_ref, bs_ref, o_ref, os_ref, xn_ref = refs
    else:
        h_ref, nw_ref, w_ref, b_ref, o_ref, xn_ref = refs
    j = pl.program_id(1)
    tm = o_ref.shape[0]
    sub = _divisor_tile(tm, FFN_PROLOGUE_ROWS, 16)
    matmul = _dot_nt if w_transposed else _dot

    @pl.when(j == 0)
    def _():
        w = w_ref[...].astype(BF16)
        if has_side:
            n_side = os_ref.shape[1]
            ws = (ws_ref[:n_side, :] if w_transposed else ws_ref[:, :n_side]).astype(BF16)
        for r0 in range(0, tm, sub):
            rows = slice(r0, r0 + sub)
            xn = (_rms_normalize(h_ref[rows, :]) * nw_ref[...]).astype(BF16)
            xn_ref[rows, :] = xn
            o_ref[rows, :] = (matmul(xn, w) + b_ref[...]).astype(o_ref.dtype)
            if has_side:
                os_ref[rows, :] = matmul(xn, ws) + bs_ref[...]

    @pl.when(j > 0)
    def _():
        o_ref[...] = (matmul(xn_ref[...], w_ref[...].astype(BF16)) + b_ref[...]).astype(o_ref.dtype)


def _norm_proj(h, nw, w, bias, n_main, b_side=None, *, w_transposed=False, tm_target=1056, tn_target=1024):
    rows, d = h.shape
    n_total = w.shape[0] if w_transposed else w.shape[1]
    tm = _divisor_tile(rows, tm_target, 16)
    tn = _divisor_tile(n_main, tn_target, 2 * LANES)
    w_block = (lambda width, col: pl.BlockSpec((width, d), lambda i, j: (col(j), 0))) if w_transposed else \
              (lambda width, col: pl.BlockSpec((d, width), lambda i, j: (0, col(j))))
    in_specs = [
        pl.BlockSpec((tm, d), lambda i, j: (i, 0)),
        pl.BlockSpec((1, d), lambda i, j: (0, 0)),
        w_block(tn, lambda j: j),
        pl.BlockSpec((1, tn), lambda i, j: (0, j)),
    ]
    out_specs = [pl.BlockSpec((tm, tn), lambda i, j: (i, j))]
    out_shape = [jax.ShapeDtypeStruct((rows, n_main), BF16)]
    args = [h, nw.reshape(1, d), w, bias.reshape(1, -1)]
    if b_side is not None:
        n_side = n_total - n_main
        assert n_main % LANES == 0 and 0 < n_side <= LANES and b_side.shape == (n_side,)
        in_specs += [w_block(LANES, lambda j: n_main // LANES), pl.BlockSpec((1, n_side), lambda i, j: (0, 0))]
        out_specs += [pl.BlockSpec((tm, n_side), lambda i, j: (i, 0))]
        out_shape += [jax.ShapeDtypeStruct((rows, n_side), F32)]
        args += [w, b_side.reshape(1, n_side)]
    return pl.pallas_call(
        functools.partial(_norm_proj_kernel, has_side=b_side is not None, w_transposed=w_transposed),
        grid=(rows // tm, n_main // tn),
        in_specs=in_specs,
        out_specs=out_specs,
        out_shape=out_shape,
        scratch_shapes=[pltpu.VMEM((tm, d), BF16)],
        compiler_params=_params("parallel", "arbitrary"),
        name="norm_proj",
    )(*args)


def _proj_residual_kernel(a_ref, w_ref, b_ref, h_ref, o_ref, wb_ref, *, tm, rows_per_batch):
    i = pl.program_id(1)

    @pl.when(i == 0)
    def _():
        wb_ref[...] = w_ref[...].astype(BF16)

    y = _dot(a_ref[...], wb_ref[...]) + b_ref[...]
    row = lax.rem(i * tm, rows_per_batch) + lax.broadcasted_iota(jnp.int32, (tm, 1), 0)
    o_ref[...] = h_ref[...] + jnp.where(_is_token_row(row, rows_per_batch - SEQ_BLOCK), y, 0.0)


PROJ_WEIGHT_TILE_BYTES = 16 * 1024 * 1024


def _proj_residual(a, w, bias, h, rows_per_batch, *, tm_target=528):
    rows, k = a.shape
    d = h.shape[1]
    tm = _divisor_tile(rows_per_batch, tm_target, 16)
    tn = _divisor_tile(d, PROJ_WEIGHT_TILE_BYTES // (4 * k), 2 * LANES)
    kern = functools.partial(_proj_residual_kernel, tm=tm, rows_per_batch=rows_per_batch)
    return pl.pallas_call(
        kern,
        grid=(d // tn, rows // tm),
        in_specs=[
            pl.BlockSpec((tm, k), lambda j, i: (i, 0)),
            pl.BlockSpec((k, tn), lambda j, i: (0, j), pipeline_mode=pl.Buffered(1)),
            pl.BlockSpec((1, tn), lambda j, i: (0, j)),
            pl.BlockSpec((tm, tn), lambda j, i: (i, j)),
        ],
        out_specs=pl.BlockSpec((tm, tn), lambda j, i: (i, j)),
        out_shape=jax.ShapeDtypeStruct((rows, d), F32),
        scratch_shapes=[pltpu.VMEM((k, tn), BF16)],
        compiler_params=_params("parallel", "arbitrary"),
        name="proj_residual",
    )(a, w, bias.reshape(1, d), h)


def _mlstm_kernel(q_ref, k_ref, v_ref, og_ref, g_ref, nw_ref, o_ref, c_ref, n_ref, m_ref):
    c = pl.program_id(1)
    blk = SEQ_BLOCK
    heads = range(MLSTM_HEADS)

    @pl.when(c == 0)
    def _():
        c_ref[...] = jnp.zeros_like(c_ref)
        n_ref[...] = jnp.zeros_like(n_ref)
        m_ref[...] = jnp.zeros_like(m_ref)

    gates = g_ref[...]
    row = c * blk + lax.broadcasted_iota(jnp.int32, (blk, 1), 0)
    valid = row >= LEAD_PAD
    log_f = jnp.where(valid, jnp.minimum(gates, 0.0) - jnp.log(1.0 + jnp.exp(-jnp.abs(gates))), 0.0)
    b_all = _cumsum_rows(log_f)
    li_all = jnp.where(valid, gates, -jnp.inf)

    ri = lax.broadcasted_iota(jnp.int32, (blk, blk), 0)
    ci = lax.broadcasted_iota(jnp.int32, (blk, blk), 1)
    eye = ri == ci
    causal = ci <= ri
    scale = MLSTM_DK ** -0.5

    q = {hd: q_ref[:, hd * MLSTM_DK:(hd + 1) * MLSTM_DK] for hd in heads}
    k = {hd: k_ref[:, hd * MLSTM_DK:(hd + 1) * MLSTM_DK] for hd in heads}
    v = {hd: v_ref[:, hd * MLSTM_DV:(hd + 1) * MLSTM_DV] for hd in heads}
    c_st = {hd: c_ref[hd] for hd in heads}
    n_st = {hd: n_ref[hd] for hd in heads}
    m_prev = {hd: m_ref[hd][:, :1] for hd in heads}

    qk_raw = {hd: _dot_nt(q[hd], k[hd]) for hd in heads}
    q_c = {hd: _dot(q[hd], c_st[hd].astype(BF16)) for hd in heads}

    b_c = {hd: b_all[:, MLSTM_HEADS + hd:MLSTM_HEADS + hd + 1] for hd in heads}
    d_c = {hd: li_all[:, hd:hd + 1] - b_c[hd] for hd in heads}
    b_last = {hd: b_c[hd][blk - 1:blk, :] for hd in heads}

    a_init, ka = {}, {}
    for hd in heads:
        log_end_init = b_last[hd] + m_prev[hd]
        log_end = b_last[hd] + d_c[hd]
        m_new = jnp.maximum(log_end_init, jnp.max(log_end, axis=0, keepdims=True))
        a_init[hd] = jnp.exp(log_end_init - m_new)
        ka[hd] = k[hd].astype(F32) * jnp.exp(log_end - m_new)
        m_ref[hd] = jnp.broadcast_to(m_new, (1, LANES))
    kv = {hd: _dot_tn(ka[hd].astype(BF16), v[hd]) for hd in heads}

    d_r = {hd: _column_to_row(d_c[hd], eye) for hd in heads}
    log_w = {hd: jnp.where(causal, b_c[hd] + d_r[hd], -jnp.inf) for hd in heads}
    log_init = {hd: b_c[hd] + m_prev[hd] for hd in heads}
    m_t = {hd: jnp.maximum(log_init[hd], jnp.max(log_w[hd], axis=-1, keepdims=True)) for hd in heads}
    w_init = {hd: jnp.exp(log_init[hd] - m_t[hd]) * scale for hd in heads}
    qk = {hd: qk_raw[hd] * (jnp.exp(log_w[hd] - m_t[hd]) * scale) for hd in heads}
    pv = {hd: _dot(qk[hd].astype(BF16), v[hd]) for hd in heads}
    qn = {hd: jnp.sum(q[hd].astype(F32) * n_st[hd], axis=-1, keepdims=True) for hd in heads}
    den = {hd: w_init[hd] * qn[hd] + jnp.sum(qk[hd], axis=-1, keepdims=True) for hd in heads}
    for hd in heads:
        hh = (w_init[hd] * q_c[hd] + pv[hd]) / jnp.maximum(jnp.abs(den[hd]), jnp.exp(-m_t[hd]))
        sl = slice(hd * MLSTM_DV, (hd + 1) * MLSTM_DV)
        gate = _sigmoid(og_ref[:, sl].astype(F32))
        o_ref[:, sl] = (_rms_normalize(hh) * nw_ref[:, sl] * gate).astype(o_ref.dtype)

    for hd in heads:
        c_ref[hd] = a_init[hd] * c_st[hd] + kv[hd]
        n_ref[hd] = a_init[hd] * n_st[hd] + jnp.sum(ka[hd], axis=0, keepdims=True)


def _mlstm_core(p_main, gates, norm_w, batch, rows_per_batch):
    blk = SEQ_BLOCK
    nblk = rows_per_batch // blk
    rows = p_main.shape[0]
    idx = lambda col: (lambda b, c: (_physical_block(b, c, nblk), col))
    return pl.pallas_call(
        _mlstm_kernel,
        grid=(batch, nblk),
        in_specs=[
            pl.BlockSpec((blk, MLSTM_QK_W), idx(0)),
            pl.BlockSpec((blk, MLSTM_QK_W), idx(1)),
            pl.BlockSpec((blk, MLSTM_V_W), idx(1)),
            pl.BlockSpec((blk, MLSTM_V_W), idx(2)),
            pl.BlockSpec((blk, 2 * MLSTM_HEADS), idx(0)),
            pl.BlockSpec((1, MLSTM_V_W), lambda b, c: (0, 0)),
        ],
        out_specs=pl.BlockSpec((blk, MLSTM_V_W), idx(0)),
        out_shape=jax.ShapeDtypeStruct((rows, MLSTM_V_W), BF16),
        scratch_shapes=[
            pltpu.VMEM((MLSTM_HEADS, MLSTM_DK, MLSTM_DV), F32),
            pltpu.VMEM((MLSTM_HEADS, 1, MLSTM_DK), F32),
            pltpu.VMEM((MLSTM_HEADS, 1, LANES), F32),
        ],
        compiler_params=_params("parallel", "arbitrary"),
        name="mlstm_core",
    )(p_main, p_main, p_main, p_main, gates, norm_w.reshape(1, MLSTM_V_W))


def _pool_kernel(h_ref, halo_ref, nw_ref, w_ref, sc_ref, o_ref, ext_ref, *, tm, rows_per_batch):
    i = pl.program_id(0)
    seq = rows_per_batch - SEQ_BLOCK
    x = h_ref[...]
    nw = nw_ref[...]
    u = _rms_normalize(x) * nw
    ext_ref[0:POOL_HALO, :] = _rms_normalize(halo_ref[...]) * nw
    ext_ref[POOL_HALO:, :] = u
    row = lax.rem(i * tm, rows_per_batch) + lax.broadcasted_iota(jnp.int32, (tm, 1), 0)
    pos = jnp.where(row < seq, row + N_META, row - (seq + LEAD_PAD))
    valid = pos >= 0
    for gi, win in enumerate(POOL_WINDOWS):
        sl = slice(gi * POOL_GROUP, (gi + 1) * POOL_GROUP)
        terms = [ext_ref[POOL_HALO - s:POOL_HALO - s + tm, sl] for s in range(win)]
        while len(terms) > 1:
            terms = [terms[a] + terms[a + 1] for a in range(0, len(terms), 2)]
        count = jnp.clip(pos + 1, 1, win).astype(F32)
        pooled = terms[0] / count - u[:, sl]
        y = _dot(pooled.astype(BF16), w_ref[gi].astype(BF16)) * sc_ref[:, sl]
        o_ref[:, sl] = x[:, sl] + jnp.where(valid, y, 0.0)


def _pool_mixer(h, nw, w_group, scale, rows_per_batch, *, tm_target=384):
    rows, d = h.shape
    tm = _divisor_tile(rows_per_batch, tm_target, POOL_HALO)
    tpb = rows_per_batch // tm
    halo_per_tile = tm // POOL_HALO
    halo_per_batch = rows_per_batch // POOL_HALO

    def halo_index(i):
        return jnp.where(i % tpb == 0, (i // tpb + 1) * halo_per_batch - 1, i * halo_per_tile - 1), 0

    kern = functools.partial(_pool_kernel, tm=tm, rows_per_batch=rows_per_batch)
    return pl.pallas_call(
        kern,
        grid=(rows // tm,),
        in_specs=[
            pl.BlockSpec((tm, d), lambda i: (i, 0)),
            pl.BlockSpec((POOL_HALO, d), halo_index),
            pl.BlockSpec((1, d), lambda i: (0, 0)),
            pl.BlockSpec(w_group.shape, lambda i: (0, 0, 0)),
            pl.BlockSpec((1, d), lambda i: (0, 0)),
        ],
        out_specs=pl.BlockSpec((tm, d), lambda i: (i, 0)),
        out_shape=jax.ShapeDtypeStruct((rows, d), F32),
        scratch_shapes=[pltpu.VMEM((tm + POOL_HALO, d), F32)],
        compiler_params=_params("parallel"),
        name="pool_mixer",
    )(h, h, nw.reshape(1, d), w_group, scale.reshape(1, d))


GDN_CARRY = 16


def _unit_lower_inverses(strict_lowers):
    n = strict_lowers[0].shape[0]
    ri = lax.broadcasted_iota(jnp.int32, (n, n), 0)
    ci = lax.broadcasted_iota(jnp.int32, (n, n), 1)
    ident = jnp.where(ri == ci, 1.0, 0.0)
    invs = [ident - l for l in strict_lowers]
    powers_b = [l.astype(BF16) for l in strict_lowers]
    terms = 2
    while terms < n:
        powers_b = [_dot(ab, ab).astype(BF16) for ab in powers_b]
        invs = [p + _dot(p.astype(BF16), ab) for p, ab in zip(invs, powers_b)]
        terms *= 2
    return invs


def _gdn_chunk(r0, gc_all, beta_all, s_ref, q_s, k_s, v_s, o_s, head_group):
    ch = GDN_CHUNK
    rows = slice(r0, r0 + ch)
    ri = lax.broadcasted_iota(jnp.int32, (ch, ch), 0)
    ci = lax.broadcasted_iota(jnp.int32, (ch, ch), 1)
    eye = ri == ci
    causal = ci <= ri
    strict = ci < ri
    rep = GDN_V_HEADS // GDN_QK_HEADS
    for g0 in range(0, GDN_V_HEADS, head_group):
        vheads = range(g0, g0 + head_group)
        qheads = range(g0 // rep, (g0 + head_group) // rep)
        q = {j: q_s[j, rows, :] for j in qheads}
        k = {j: k_s[j, rows, :] for j in qheads}
        kk = {j: _dot_nt(k[j], k[j]) for j in qheads}
        qk = {j: _dot_nt(q[j], k[j]) for j in qheads}
        gc_c = {jv: gc_all[:, jv:jv + 1] for jv in vheads}
        beta_c = {jv: beta_all[:, jv:jv + 1] for jv in vheads}
        decay = {jv: jnp.exp(jnp.where(causal, gc_c[jv] - _column_to_row(gc_c[jv], eye), -jnp.inf)) for jv in vheads}
        t_inv = _unit_lower_inverses([jnp.where(strict, kk[jv // rep] * decay[jv], 0.0) * beta_c[jv] for jv in vheads])
        t_inv = dict(zip(vheads, t_inv))
        egc = {jv: jnp.exp(gc_c[jv]) for jv in vheads}
        s_st = {jv: s_ref[jv] for jv in vheads}
        s_b = {jv: s_st[jv].astype(BF16) for jv in vheads}
        qs = {jv: _dot(q[jv // rep], s_b[jv]) * egc[jv] for jv in vheads}
        rhs = {jv: jnp.concatenate([v_s[jv, rows, :].astype(F32) * beta_c[jv],
                                    k[jv // rep].astype(F32) * (beta_c[jv] * egc[jv])], axis=-1).astype(BF16)
               for jv in vheads}
        sol = {jv: _dot(t_inv[jv].astype(BF16), rhs[jv]) for jv in vheads}
        v_new = {jv: sol[jv][:, :GDN_DV] - _dot(sol[jv][:, GDN_DV:].astype(BF16), s_b[jv]) for jv in vheads}
        for jv in vheads:
            o_s[jv, rows, :] = qs[jv] + _dot((qk[jv // rep] * decay[jv]).astype(BF16), v_new[jv].astype(BF16))
        for jv in vheads:
            g_last = gc_c[jv][ch - 1:ch, :]
            v_dec = (v_new[jv] * jnp.exp(g_last - gc_c[jv])).astype(BF16)
            s_ref[jv] = jnp.exp(g_last) * s_st[jv] + _dot_tn(k[jv // rep], v_dec)


def _gdn_kernel(x_ref, z_ref, e_ref, cw_ref, al_ref, dt_ref, nw_ref, o_ref,
                ext_ref, s_ref, q_s, k_s, v_s, o_s, *, head_group):
    c = pl.program_id(1)
    blk = SEQ_BLOCK

    @pl.when(c == 0)
    def _():
        ext_ref[blk:, :] = jnp.zeros((GDN_CARRY, GDN_CONV_CH), BF16)
        s_ref[...] = jnp.zeros_like(s_ref)

    ext_ref[0:blk, :] = x_ref[...]
    n_shift = GDN_CONV - 1
    r = lax.broadcasted_iota(jnp.int32, (n_shift * blk, blk + GDN_CARRY), 0)
    src_col = lax.broadcasted_iota(jnp.int32, (n_shift * blk, blk + GDN_CARRY), 1)
    shift = 1 + (r >= blk).astype(jnp.int32) + (r >= 2 * blk).astype(jnp.int32)
    t = r - (shift - 1) * blk
    src = jnp.where(t >= shift, t - shift, blk + GDN_CARRY + t - shift)
    select_rows = jnp.where(src_col == src, 1.0, 0.0).astype(BF16)
    for cb2 in range(GDN_CONV_CH // (2 * LANES)):
        shifted = _dot(select_rows, ext_ref[:, cb2 * 2 * LANES:(cb2 + 1) * 2 * LANES])
        for half in range(2):
            cb = 2 * cb2 + half
            sl = slice(cb * LANES, (cb + 1) * LANES)
            acc = x_ref[:, sl].astype(F32) * cw_ref[n_shift:n_shift + 1, sl]
            for s in range(1, GDN_CONV):
                x_back = shifted[(s - 1) * blk:s * blk, half * LANES:(half + 1) * LANES]
                acc = acc + x_back * cw_ref[n_shift - s:n_shift - s + 1, sl]
            y = acc * _sigmoid(acc)
            if cb < 2 * GDN_QK_HEADS:
                y = y * lax.rsqrt(jnp.sum(y * y, axis=-1, keepdims=True) + EPS)
                if cb < GDN_QK_HEADS:
                    q_s[cb] = (y * (GDN_DK ** -0.5)).astype(BF16)
                else:
                    k_s[cb - GDN_QK_HEADS] = y.astype(BF16)
            else:
                v_s[cb - 2 * GDN_QK_HEADS] = y.astype(BF16)
    ext_ref[blk:, :] = x_ref[blk - GDN_CARRY:, :]

    row = c * blk + lax.broadcasted_iota(jnp.int32, (blk, 1), 0)
    valid = row >= LEAD_PAD
    e = e_ref[...]
    beta_all = jnp.where(valid, _sigmoid(e[:, :GDN_V_HEADS]), 0.0)
    g_all = jnp.where(valid, -jnp.exp(al_ref[...]) * _softplus(e[:, GDN_V_HEADS:] + dt_ref[...]), 0.0)

    for r0 in range(0, blk, GDN_CHUNK):
        rows = slice(r0, r0 + GDN_CHUNK)
        _gdn_chunk(r0, _cumsum_rows(g_all[rows]), beta_all[rows], s_ref, q_s, k_s, v_s, o_s, head_group)

    for jv in range(GDN_V_HEADS):
        sl = slice(jv * GDN_DV, (jv + 1) * GDN_DV)
        z = z_ref[:, sl].astype(F32)
        o_ref[:, sl] = (_rms_normalize(o_s[jv]) * nw_ref[...] * (z * _sigmoid(z))).astype(o_ref.dtype)


def _gdn_core(p_main, side, conv_w, a_log, dt_bias, norm_w, batch, rows_per_batch, *, head_group=32):
    blk = SEQ_BLOCK
    nblk = rows_per_batch // blk
    rows = p_main.shape[0]
    idx = lambda col: (lambda b, c: (_physical_block(b, c, nblk), col))
    const2 = lambda b, c: (0, 0)
    kern = functools.partial(_gdn_kernel, head_group=head_group)
    return pl.pallas_call(
        kern,
        grid=(batch, nblk),
        in_specs=[
            pl.BlockSpec((blk, GDN_CONV_CH), idx(0)),
            pl.BlockSpec((blk, GDN_V_W), idx(GDN_CONV_CH // GDN_V_W)),
            pl.BlockSpec((blk, 2 * GDN_V_HEADS), idx(0)),
            pl.BlockSpec((GDN_CONV, GDN_CONV_CH), const2),
            pl.BlockSpec((1, GDN_V_HEADS), const2),
            pl.BlockSpec((1, GDN_V_HEADS), const2),
            pl.BlockSpec((1, GDN_DV), const2),
        ],
        out_specs=pl.BlockSpec((blk, GDN_V_W), idx(0)),
        out_shape=jax.ShapeDtypeStruct((rows, GDN_V_W), BF16),
        scratch_shapes=[
            pltpu.VMEM((blk + GDN_CARRY, GDN_CONV_CH), BF16),
            pltpu.VMEM((GDN_V_HEADS, GDN_DK, GDN_DV), F32),
            pltpu.VMEM((GDN_QK_HEADS, blk, GDN_DK), BF16),
            pltpu.VMEM((GDN_QK_HEADS, blk, GDN_DK), BF16),
            pltpu.VMEM((GDN_V_HEADS, blk, GDN_DV), BF16),
            pltpu.VMEM((GDN_V_HEADS, blk, GDN_DV), F32),
        ],
        compiler_params=_params("parallel", "arbitrary"),
        name="gdn_core",
    )(p_main, p_main, side, conv_w, a_log.reshape(1, -1), dt_bias.reshape(1, -1), norm_w.reshape(1, -1))


def _rope_table_kernel(cos_ref, sin_ref):
    shape = cos_ref.shape
    pos = (lax.broadcasted_iota(jnp.int32, shape, 0) - LEAD_PAD).astype(F32)
    pair = lax.rem(lax.broadcasted_iota(jnp.int32, shape, 1), SWA_DH // 2).astype(F32)
    inv = jnp.exp(pair * (-2.0 * math.log(ROPE_THETA) / SWA_DH))
    ang = pos * inv
    cos_ref[...] = jnp.cos(ang)
    sin_ref[...] = jnp.sin(ang)


def _rope_tables(rows_per_batch):
    shape = jax.ShapeDtypeStruct((rows_per_batch, LANES), F32)
    return pl.pallas_call(_rope_table_kernel, out_shape=[shape, shape], name="rope_tables")()


SWA_COL_BATCH = 4


def _swa_kernel(sink_ref, q_ref, kv_ref, kvp_ref, cos_ref, sin_ref, cosp_ref, sinp_ref, o_ref):
    n = pl.program_id(1)
    blk = SEQ_BLOCK
    lane = lax.broadcasted_iota(jnp.int32, (1, LANES), 1)
    low = lane < SWA_DH
    first_half = lax.rem(lane, SWA_DH) < SWA_DH // 2

    def rope(x, cos, sin):
        rot = jnp.where(first_half, -pltpu.roll(x, LANES - SWA_DH // 2, 1), pltpu.roll(x, SWA_DH // 2, 1))
        return x * cos + rot * sin

    cos, sin = cos_ref[...], sin_ref[...]
    cosp, sinp = cosp_ref[...], sinp_ref[...]

    stacked = lax.broadcasted_iota(jnp.int32, (2 * blk, 2 * blk), 0)
    q_row = n * blk + jnp.where(stacked < blk, stacked, stacked - blk)
    k_row = (n - 1) * blk + lax.broadcasted_iota(jnp.int32, (2 * blk, 2 * blk), 1)
    mask = (k_row <= q_row) & (q_row - k_row < SWA_WINDOW) & (k_row >= LEAD_PAD)
    top = lax.broadcasted_iota(jnp.int32, (2 * blk, 1), 0) < blk

    def both_heads(x, head_in_pair):
        swapped = pltpu.roll(x, SWA_DH, 1)
        return jnp.where(low, x, swapped) if head_in_pair == 0 else jnp.where(low, swapped, x)

    kk, v_stack = {}, {}
    for g in range(SWA_HKV):
        pair_col, head_in_pair = divmod(g, 2)
        ksl = slice(pair_col * LANES, (pair_col + 1) * LANES)
        vsl = slice(SWA_KV_W + pair_col * LANES, SWA_KV_W + (pair_col + 1) * LANES)
        k_cur = both_heads(rope(kv_ref[:, ksl].astype(F32), cos, sin), head_in_pair)
        k_prev = both_heads(rope(kvp_ref[:, ksl].astype(F32), cosp, sinp), head_in_pair)
        kk[g] = jnp.concatenate([k_prev, k_cur], axis=0).astype(BF16)
        v_cur = both_heads(kv_ref[:, vsl].astype(F32), head_in_pair)
        v_prev = both_heads(kvp_ref[:, vsl].astype(F32), head_in_pair)
        v2 = jnp.concatenate([v_prev, v_cur], axis=0)
        v_stack[g] = jnp.concatenate([jnp.where(low, v2, 0.0), jnp.where(low, 0.0, v2)], axis=0).astype(BF16)

    def scores(col):
        qp = rope(q_ref[:, col * LANES:(col + 1) * LANES].astype(F32), cos, sin)
        q2 = jnp.concatenate([jnp.where(low, qp, 0.0), jnp.where(low, 0.0, qp)], axis=0).astype(BF16)
        s = _dot_nt(q2, kk[col // (SWA_GROUP // 2)]) * (SWA_DH ** -0.5)
        return jnp.where(mask, s, -jnp.inf)

    def attend(cols, s):
        sink = {c: jnp.where(top, sink_ref[2 * c], sink_ref[2 * c + 1]) for c in cols}
        m = {c: jnp.maximum(jnp.max(s[c], axis=-1, keepdims=True), sink[c]) for c in cols}
        e = {c: jnp.exp(s[c] - m[c]) for c in cols}
        inv = {c: 1.0 / (jnp.sum(e[c], axis=-1, keepdims=True) + jnp.exp(sink[c] - m[c])) for c in cols}
        for c in cols:
            prob = e[c] * inv[c]
            p2 = jnp.concatenate([prob[:blk], prob[blk:]], axis=1).astype(BF16)
            o_ref[:, c * LANES:(c + 1) * LANES] = _dot(p2, v_stack[c // (SWA_GROUP // 2)]).astype(o_ref.dtype)

    n_cols = SWA_HQ // 2
    batches = [range(c0, c0 + SWA_COL_BATCH) for c0 in range(0, n_cols, SWA_COL_BATCH)]
    pending = {c: scores(c) for c in batches[0]}
    for bi, cols in enumerate(batches):
        current = {c: pending.pop(c) for c in cols}
        if bi + 1 < len(batches):
            pending.update({c: scores(c) for c in batches[bi + 1]})
        attend(cols, current)


def _swa_core(p_main, sinks, cos, sin, batch, rows_per_batch):
    blk = SEQ_BLOCK
    nblk = rows_per_batch // blk
    rows = p_main.shape[0]
    kv_col = SWA_Q_W // (2 * SWA_KV_W)
    cur = lambda col: (lambda b, n: (_physical_block(b, n, nblk), col))
    prev = lambda col: (lambda b, n: (_physical_block(b, jnp.maximum(n - 1, 0), nblk), col))
    tab_cur = lambda b, n: (n, 0)
    tab_prev = lambda b, n: (jnp.maximum(n - 1, 0), 0)
    return pl.pallas_call(
        _swa_kernel,
        grid=(batch, nblk),
        in_specs=[
            pl.BlockSpec(memory_space=pltpu.SMEM),
            pl.BlockSpec((blk, SWA_Q_W), cur(0)),
            pl.BlockSpec((blk, 2 * SWA_KV_W), cur(kv_col)),
            pl.BlockSpec((blk, 2 * SWA_KV_W), prev(kv_col)),
            pl.BlockSpec((blk, LANES), tab_cur),
            pl.BlockSpec((blk, LANES), tab_cur),
            pl.BlockSpec((blk, LANES), tab_prev),
            pl.BlockSpec((blk, LANES), tab_prev),
        ],
        out_specs=pl.BlockSpec((blk, SWA_Q_W), cur(0)),
        out_shape=jax.ShapeDtypeStruct((rows, SWA_Q_W), BF16),
        compiler_params=_params("parallel", "parallel"),
        name="swa_core",
    )(sinks, p_main, p_main, p_main, cos, sin, cos, sin)


def kernel(x, meta_tokens, norm_w, ffn_w_gate, ffn_w_up, ffn_w_down, mlstm_w_in, mlstm_b_if, mlstm_norm_w, mlstm_w_out, pool_w, pool_scale, gdn_w_in, gdn_conv_w, gdn_a_log, gdn_dt_bias, gdn_norm_w, gdn_w_out, swa_w_qkv, swa_b_qkv, swa_sinks, swa_w_out, swa_b_out, final_norm_w):
    batch, seq, d = x.shape
    rpb = _rows_per_batch(seq)
    zeros_d = jnp.zeros((d,), F32)
    h = x
    for i in range(DEPTH):
        m, j = i % N_MIXERS, i // N_MIXERS
        h = _ffn(h, norm_w[i, 0], ffn_w_gate, ffn_w_up, ffn_w_down, i, 0, rpb,
                 meta=meta_tokens.astype(x.dtype) if i == 0 else None)
        h = h.reshape(batch * rpb, d)
        if m == 0:
            p_main, gates = _norm_proj(h, norm_w[i, 1], mlstm_w_in[j].T, jnp.zeros((MLSTM_MAIN_W,), F32),
                                       MLSTM_MAIN_W, mlstm_b_if[j], w_transposed=True)
            a = _mlstm_core(p_main, gates, mlstm_norm_w[j], batch, rpb)
            h = _proj_residual(a, mlstm_w_out[j], zeros_d, h, rpb)
        elif m == 1:
            h = _pool_mixer(h, norm_w[i, 1], pool_w[j], pool_scale[j], rpb)
        elif m == 2:
            p_main, side = _norm_proj(h, norm_w[i, 1], gdn_w_in[j].T, jnp.zeros((GDN_MAIN_W,), F32), GDN_MAIN_W,
                                      jnp.zeros((2 * GDN_V_HEADS,), F32), w_transposed=True)
            a = _gdn_core(p_main, side, gdn_conv_w[j], gdn_a_log[j], gdn_dt_bias[j], gdn_norm_w[j], batch, rpb)
            h = _proj_residual(a, gdn_w_out[j], zeros_d, h, rpb)
        else:
            (p_main,) = _norm_proj(h, norm_w[i, 1], swa_w_qkv[j], swa_b_qkv[j], SWA_IN)
            cos, sin = _rope_tables(rpb)
            a = _swa_core(p_main, swa_sinks[j], cos, sin, batch, rpb)
            h = _proj_residual(a, swa_w_out[j], swa_b_out[j], h, rpb)
        h = _ffn(h.reshape(batch, rpb, d), norm_w[i, 2], ffn_w_gate, ffn_w_up, ffn_w_down, i, 1, rpb,
                 final_w=final_norm_w if i == DEPTH - 1 else None)
    return h
```

```python
import functools
import math

import jax
import jax.numpy as jnp
from jax import lax
from jax.experimental import pallas as pl
from jax.experimental.pallas import tpu as pltpu

F32 = jnp.float32
BF16 = jnp.bfloat16

D_MODEL = 2048
DEPTH = 4
N_MIXERS = 4
N_META = 16
EPS = 1e-6

MLSTM_HEADS = 8
MLSTM_DV = 256
MLSTM_DK = 128
MLSTM_QK_W = MLSTM_HEADS * MLSTM_DK
MLSTM_V_W = MLSTM_HEADS * MLSTM_DV
MLSTM_MAIN_W = 2 * MLSTM_QK_W + 2 * MLSTM_V_W

POOL_WINDOWS = (2, 4, 8, 16)
POOL_GROUP = D_MODEL // len(POOL_WINDOWS)
POOL_HALO = 16

GDN_DK = 128
GDN_DV = 128
GDN_QK_HEADS = 16
GDN_V_HEADS = 32
GDN_CONV = 4
GDN_CHUNK = 64
GDN_QK_W = GDN_QK_HEADS * GDN_DK
GDN_V_W = GDN_V_HEADS * GDN_DV
GDN_CONV_CH = 2 * GDN_QK_W + GDN_V_W
GDN_MAIN_W = GDN_CONV_CH + GDN_V_W

SWA_DH = 64
SWA_HQ = 32
SWA_GROUP = 8
SWA_HKV = 4
SWA_WINDOW = 128
SWA_Q_W = SWA_HQ * SWA_DH
SWA_KV_W = SWA_HKV * SWA_DH
SWA_IN = SWA_Q_W + 2 * SWA_KV_W
ROPE_THETA = 10000.0

LANES = 128
SEQ_BLOCK = 128
LEAD_PAD = SEQ_BLOCK - N_META

VMEM_LIMIT_BYTES = 60 * 1024 * 1024
FFN_WEIGHT_SLOTS = 3
FFN_PROLOGUE_ROWS = 384


def _rows_per_batch(seq):
    assert seq % SEQ_BLOCK == 0, seq
    return seq + SEQ_BLOCK


def _divisor_tile(total, target, multiple):
    best = None
    for t in range(multiple, min(total, target) + 1, multiple):
        if total % t == 0:
            best = t
    assert best is not None, (total, target, multiple)
    return best


def _params(*semantics):
    return pltpu.CompilerParams(dimension_semantics=semantics, vmem_limit_bytes=VMEM_LIMIT_BYTES)


def _physical_block(b, n, nblk):
    return b * nblk + jnp.where(n == 0, nblk - 1, n - 1)


def _is_token_row(row_in_batch, seq):
    return (row_in_batch < seq) | (row_in_batch >= seq + LEAD_PAD)


def _rms_normalize(x):
    return x * lax.rsqrt(jnp.mean(x * x, axis=-1, keepdims=True) + EPS)


def _sigmoid(x):
    return 1.0 / (1.0 + jnp.exp(-x))


def _softplus(x):
    return jnp.maximum(x, 0.0) + jnp.log(1.0 + jnp.exp(-jnp.abs(x)))


def _dot(a, b):
    return jnp.dot(a, b, preferred_element_type=F32)


def _dot_nt(a, b):
    return lax.dot_general(a, b, (((1,), (1,)), ((), ())), preferred_element_type=F32)


def _dot_tn(a, b):
    return lax.dot_general(a, b, (((0,), (0,)), ((), ())), preferred_element_type=F32)


def _cumsum_rows(x):
    n = x.shape[0]
    row = lax.broadcasted_iota(jnp.int32, x.shape, 0)
    shift = 1
    while shift < n:
        x = x + jnp.where(row >= shift, pltpu.roll(x, shift, 0), 0.0)
        shift *= 2
    return x


def _column_to_row(col, eye):
    return jnp.sum(jnp.where(eye, col, 0.0), axis=0, keepdims=True)


def _ffn_kernel(*refs, assemble, final_norm, tiles_per_batch, layer, half):
    h_ref, nw_ref, wg_hbm, wu_hbm, wd_hbm = refs[:5]
    rest = list(refs[5:])
    meta_ref = rest.pop(0) if assemble else None
    fw_ref = rest.pop(0) if final_norm else None
    o_ref, xn_ref, wg_buf, wu_buf, wd_buf, w_sem = rest
    j = pl.program_id(1)
    n_j = pl.num_programs(1)

    tm = o_ref.shape[0]
    tf = wg_buf.shape[2]
    sub = _divisor_tile(tm, FFN_PROLOGUE_ROWS, 16)
    assert sub >= SEQ_BLOCK

    def weight_copies(step, slot):
        col = pl.multiple_of(lax.rem(step, n_j) * tf, tf)
        return (
            pltpu.make_async_copy(wg_hbm.at[layer, half, :, pl.ds(col, tf)], wg_buf.at[slot], w_sem.at[0, slot]),
            pltpu.make_async_copy(wu_hbm.at[layer, half, :, pl.ds(col, tf)], wu_buf.at[slot], w_sem.at[1, slot]),
            pltpu.make_async_copy(wd_hbm.at[layer, half, pl.ds(col, tf), :], wd_buf.at[slot], w_sem.at[2, slot]),
        )

    step = pl.program_id(0) * n_j + j
    n_steps = pl.num_programs(0) * n_j
    ahead = FFN_WEIGHT_SLOTS - 1

    @pl.when(step == 0)
    def _():
        for k in range(ahead):
            for cp in weight_copies(k, k):
                cp.start()

    @pl.when(step + ahead < n_steps)
    def _():
        for cp in weight_copies(step + ahead, lax.rem(step + ahead, FFN_WEIGHT_SLOTS)):
            cp.start()

    slot = lax.rem(step, FFN_WEIGHT_SLOTS)
    for cp in weight_copies(step, slot):
        cp.wait()
    wg_ref, wu_ref, wd_ref = wg_buf.at[slot], wu_buf.at[slot], wd_buf.at[slot]

    def swiglu_rows(r0, r1, wg, wu, wd):
        xn = xn_ref[r0:r1, :]
        g = _dot(xn, wg)
        u = _dot(xn, wu)
        a = (0.5 * g * _sigmoid(g) * u).astype(BF16)
        o_ref[r0:r1, :] += _dot(a, wd)

    @pl.when(j == 0)
    def _():
        body_end = tm
        if assemble:
            body_end = tm - SEQ_BLOCK
            ends_batch = lax.rem(pl.program_id(0), tiles_per_batch) == tiles_per_batch - 1

            @pl.when(ends_batch)
            def _():
                o_ref[body_end:tm - N_META, :] = jnp.zeros((LEAD_PAD, o_ref.shape[1]), F32)
                o_ref[tm - N_META:, :] = meta_ref[...]

            @pl.when(jnp.logical_not(ends_batch))
            def _():
                o_ref[body_end:, :] = h_ref[body_end:, :]

        wg, wu, wd = (w[...].astype(BF16) for w in (wg_ref, wu_ref, wd_ref))
        for r0 in range(0, tm, sub):
            r1 = r0 + sub
            rc = min(r1, body_end)
            if rc > r0:
                o_ref[r0:rc, :] = h_ref[r0:rc, :]
            xn_ref[r0:r1, :] = (_rms_normalize(o_ref[r0:r1, :]) * nw_ref[...]).astype(BF16)
            swiglu_rows(r0, r1, wg, wu, wd)

    @pl.when(j > 0)
    def _():
        swiglu_rows(0, tm, *(w[...].astype(BF16) for w in (wg_ref, wu_ref, wd_ref)))

    if final_norm:
        @pl.when(j == pl.num_programs(1) - 1)
        def _():
            o_ref[...] = _rms_normalize(o_ref[...]) * fw_ref[...]


def _ffn(h, nw, w_gate, w_up, w_down, layer, half, rows_per_batch, *, meta=None, final_w=None,
         tm_target=1056, tf=256):
    batch, _, d = h.shape
    d_ff = w_gate.shape[-1]
    tm = _divisor_tile(rows_per_batch, tm_target, SEQ_BLOCK // 8)
    assert tm >= SEQ_BLOCK and d_ff % tf == 0
    tpb = rows_per_batch // tm
    assert batch * tpb * (d_ff // tf) >= FFN_WEIGHT_SLOTS
    row_spec = pl.BlockSpec((None, tm, d), lambda i, j: (i // tpb, i % tpb, 0))
    vec_spec = pl.BlockSpec((1, d), lambda i, j: (0, 0))
    hbm_spec = pl.BlockSpec(memory_space=pl.ANY)
    in_specs = [row_spec, vec_spec, hbm_spec, hbm_spec, hbm_spec]
    args = [h, nw.reshape(1, d), w_gate, w_up, w_down]
    if meta is not None:
        in_specs.append(pl.BlockSpec((N_META, d), lambda i, j: (0, 0)))
        args.append(meta)
    if final_w is not None:
        in_specs.append(vec_spec)
        args.append(final_w.reshape(1, d))
    out_rows = rows_per_batch - SEQ_BLOCK if final_w is not None else rows_per_batch
    kern = functools.partial(_ffn_kernel, assemble=meta is not None, final_norm=final_w is not None,
                             tiles_per_batch=tpb, layer=layer, half=half)
    return pl.pallas_call(
        kern,
        grid=(batch * tpb, d_ff // tf),
        in_specs=in_specs,
        out_specs=row_spec,
        out_shape=jax.ShapeDtypeStruct((batch, out_rows, d), F32),
        scratch_shapes=[
            pltpu.VMEM((tm, d), BF16),
            pltpu.VMEM((FFN_WEIGHT_SLOTS, d, tf), w_gate.dtype),
            pltpu.VMEM((FFN_WEIGHT_SLOTS, d, tf), w_up.dtype),
            pltpu.VMEM((FFN_WEIGHT_SLOTS, tf, d), w_down.dtype),
            pltpu.SemaphoreType.DMA((3, FFN_WEIGHT_SLOTS)),
        ],
        compiler_params=_params("arbitrary", "arbitrary"),
        name="ffn",
    )(*args)


def _norm_proj_kernel(*refs, has_side, w_transposed):
    if has_side:
        h_ref, nw_ref, w_ref, b_ref, ws_ref, bs_ref, o_ref, os_ref, xn_ref = refs
    else:
        h_ref, nw_ref, w_ref, b_ref, o_ref, xn_ref = refs
    j = pl.program_id(1)
    tm = o_ref.shape[0]
    sub = _divisor_tile(tm, FFN_PROLOGUE_ROWS, 16)
    matmul = _dot_nt if w_transposed else _dot

    @pl.when(j == 0)
    def _():
        w = w_ref[...].astype(BF16)
        if has_side:
            n_side = os_ref.shape[1]
            ws = (ws_ref[:n_side, :] if w_transposed else ws_ref[:, :n_side]).astype(BF16)
        for r0 in range(0, tm, sub):
            rows = slice(r0, r0 + sub)
            xn = (_rms_normalize(h_ref[rows, :]) * nw_ref[...]).astype(BF16)
            xn_ref[rows, :] = xn
            o_ref[rows, :] = (matmul(xn, w) + b_ref[...]).astype(o_ref.dtype)
            if has_side:
                os_ref[rows, :] = matmul(xn, ws) + bs_ref[...]

    @pl.when(j > 0)
    def _():
        o_ref[...] = (matmul(xn_ref[...], w_ref[...].astype(BF16)) + b_ref[...]).astype(o_ref.dtype)


def _norm_proj(h, nw, w, bias, n_main, b_side=None, *, w_transposed=False, tm_target=1056, tn_target=1024):
    rows, d = h.shape
    n_total = w.shape[0] if w_transposed else w.shape[1]
    tm = _divisor_tile(rows, tm_target, 16)
    tn = _divisor_tile(n_main, tn_target, 2 * LANES)
    w_block = (lambda width, col: pl.BlockSpec((width, d), lambda i, j: (col(j), 0))) if w_transposed else \
              (lambda width, col: pl.BlockSpec((d, width), lambda i, j: (0, col(j))))
    in_specs = [
        pl.BlockSpec((tm, d), lambda i, j: (i, 0)),
        pl.BlockSpec((1, d), lambda i, j: (0, 0)),
        w_block(tn, lambda j: j),
        pl.BlockSpec((1, tn), lambda i, j: (0, j)),
    ]
    out_specs = [pl.BlockSpec((tm, tn), lambda i, j: (i, j))]
    out_shape = [jax.ShapeDtypeStruct((rows, n_main), BF16)]
    args = [h, nw.reshape(1, d), w, bias.reshape(1, -1)]
    if b_side is not None:
        n_side = n_total - n_main
        assert n_main % LANES == 0 and 0 < n_side <= LANES and b_side.shape == (n_side,)
        in_specs += [w_block(LANES, lambda j: n_main // LANES), pl.BlockSpec((1, n_side), lambda i, j: (0, 0))]
        out_specs += [pl.BlockSpec((tm, n_side), lambda i, j: (i, 0))]
        out_shape += [jax.ShapeDtypeStruct((rows, n_side), F32)]
        args += [w, b_side.reshape(1, n_side)]
    return pl.pallas_call(
        functools.partial(_norm_proj_kernel, has_side=b_side is not None, w_transposed=w_transposed),
        grid=(rows // tm, n_main // tn),
        in_specs=in_specs,
        out_specs=out_specs,
        out_shape=out_shape,
        scratch_shapes=[pltpu.VMEM((tm, d), BF16)],
        compiler_params=_params("parallel", "arbitrary"),
        name="norm_proj",
    )(*args)


def _proj_residual_kernel(a_ref, w_ref, b_ref, h_ref, o_ref, wb_ref, *, tm, rows_per_batch):
    i = pl.program_id(1)

    @pl.when(i == 0)
    def _():
        wb_ref[...] = w_ref[...].astype(BF16)

    y = _dot(a_ref[...], wb_ref[...]) + b_ref[...]
    row = lax.rem(i * tm, rows_per_batch) + lax.broadcasted_iota(jnp.int32, (tm, 1), 0)
    o_ref[...] = h_ref[...] + jnp.where(_is_token_row(row, rows_per_batch - SEQ_BLOCK), y, 0.0)


PROJ_WEIGHT_TILE_BYTES = 16 * 1024 * 1024


def _proj_residual(a, w, bias, h, rows_per_batch, *, tm_target=528):
    rows, k = a.shape
    d = h.shape[1]
    tm = _divisor_tile(rows_per_batch, tm_target, 16)
    tn = _divisor_tile(d, PROJ_WEIGHT_TILE_BYTES // (4 * k), 2 * LANES)
    kern = functools.partial(_proj_residual_kernel, tm=tm, rows_per_batch=rows_per_batch)
    return pl.pallas_call(
        kern,
        grid=(d // tn, rows // tm),
        in_specs=[
            pl.BlockSpec((tm, k), lambda j, i: (i, 0)),
            pl.BlockSpec((k, tn), lambda j, i: (0, j), pipeline_mode=pl.Buffered(1)),
            pl.BlockSpec((1, tn), lambda j, i: (0, j)),
            pl.BlockSpec((tm, tn), lambda j, i: (i, j)),
        ],
        out_specs=pl.BlockSpec((tm, tn), lambda j, i: (i, j)),
        out_shape=jax.ShapeDtypeStruct((rows, d), F32),
        scratch_shapes=[pltpu.VMEM((k, tn), BF16)],
        compiler_params=_params("parallel", "arbitrary"),
        name="proj_residual",
    )(a, w, bias.reshape(1, d), h)


def _mlstm_kernel(q_ref, k_ref, v_ref, og_ref, g_ref, nw_ref, o_ref, c_ref, n_ref, m_ref):
    c = pl.program_id(1)
    blk = SEQ_BLOCK
    heads = range(MLSTM_HEADS)

    @pl.when(c == 0)
    def _():
        c_ref[...] = jnp.zeros_like(c_ref)
        n_ref[...] = jnp.zeros_like(n_ref)
        m_ref[...] = jnp.zeros_like(m_ref)

    gates = g_ref[...]
    row = c * blk + lax.broadcasted_iota(jnp.int32, (blk, 1), 0)
    valid = row >= LEAD_PAD
    log_f = jnp.where(valid, jnp.minimum(gates, 0.0) - jnp.log(1.0 + jnp.exp(-jnp.abs(gates))), 0.0)
    b_all = _cumsum_rows(log_f)
    li_all = jnp.where(valid, gates, -jnp.inf)

    ri = lax.broadcasted_iota(jnp.int32, (blk, blk), 0)
    ci = lax.broadcasted_iota(jnp.int32, (blk, blk), 1)
    eye = ri == ci
    causal = ci <= ri
    scale = MLSTM_DK ** -0.5

    q = {hd: q_ref[:, hd * MLSTM_DK:(hd + 1) * MLSTM_DK] for hd in heads}
    k = {hd: k_ref[:, hd * MLSTM_DK:(hd + 1) * MLSTM_DK] for hd in heads}
    v = {hd: v_ref[:, hd * MLSTM_DV:(hd + 1) * MLSTM_DV] for hd in heads}
    c_st = {hd: c_ref[hd] for hd in heads}
    n_st = {hd: n_ref[hd] for hd in heads}
    m_prev = {hd: m_ref[hd][:, :1] for hd in heads}

    qk_raw = {hd: _dot_nt(q[hd], k[hd]) for hd in heads}
    q_c = {hd: _dot(q[hd], c_st[hd].astype(BF16)) for hd in heads}

    b_c = {hd: b_all[:, MLSTM_HEADS + hd:MLSTM_HEADS + hd + 1] for hd in heads}
    d_c = {hd: li_all[:, hd:hd + 1] - b_c[hd] for hd in heads}
    b_last = {hd: b_c[hd][blk - 1:blk, :] for hd in heads}

    a_init, ka = {}, {}
    for hd in heads:
        log_end_init = b_last[hd] + m_prev[hd]
        log_end = b_last[hd] + d_c[hd]
        m_new = jnp.maximum(log_end_init, jnp.max(log_end, axis=0, keepdims=True))
        a_init[hd] = jnp.exp(log_end_init - m_new)
        ka[hd] = k[hd].astype(F32) * jnp.exp(log_end - m_new)
        m_ref[hd] = jnp.broadcast_to(m_new, (1, LANES))
    kv = {hd: _dot_tn(ka[hd].astype(BF16), v[hd]) for hd in heads}

    d_r = {hd: _column_to_row(d_c[hd], eye) for hd in heads}
    log_w = {hd: jnp.where(causal, b_c[hd] + d_r[hd], -jnp.inf) for hd in heads}
    log_init = {hd: b_c[hd] + m_prev[hd] for hd in heads}
    m_t = {hd: jnp.maximum(log_init[hd], jnp.max(log_w[hd], axis=-1, keepdims=True)) for hd in heads}
    w_init = {hd: jnp.exp(log_init[hd] - m_t[hd]) * scale for hd in heads}
    qk = {hd: qk_raw[hd] * (jnp.exp(log_w[hd] - m_t[hd]) * scale) for hd in heads}
    pv = {hd: _dot(qk[hd].astype(BF16), v[hd]) for hd in heads}
    qn = {hd: jnp.sum(q[hd].astype(F32) * n_st[hd], axis=-1, keepdims=True) for hd in heads}
    den = {hd: w_init[hd] * qn[hd] + jnp.sum(qk[hd], axis=-1, keepdims=True) for hd in heads}
    for hd in heads:
        hh = (w_init[hd] * q_c[hd] + pv[hd]) / jnp.maximum(jnp.abs(den[hd]), jnp.exp(-m_t[hd]))
        sl = slice(hd * MLSTM_DV, (hd + 1) * MLSTM_DV)
        gate = _sigmoid(og_ref[:, sl].astype(F32))
        o_ref[:, sl] = (_rms_normalize(hh) * nw_ref[:, sl] * gate).astype(o_ref.dtype)

    for hd in heads:
        c_ref[hd] = a_init[hd] * c_st[hd] + kv[hd]
        n_ref[hd] = a_init[hd] * n_st[hd] + jnp.sum(ka[hd], axis=0, keepdims=True)


def _mlstm_core(p_main, gates, norm_w, batch, rows_per_batch):
    blk = SEQ_BLOCK
    nblk = rows_per_batch // blk
    rows = p_main.shape[0]
    idx = lambda col: (lambda b, c: (_physical_block(b, c, nblk), col))
    return pl.pallas_call(
        _mlstm_kernel,
        grid=(batch, nblk),
        in_specs=[
            pl.BlockSpec((blk, MLSTM_QK_W), idx(0)),
            pl.BlockSpec((blk, MLSTM_QK_W), idx(1)),
            pl.BlockSpec((blk, MLSTM_V_W), idx(1)),
            pl.BlockSpec((blk, MLSTM_V_W), idx(2)),
            pl.BlockSpec((blk, 2 * MLSTM_HEADS), idx(0)),
            pl.BlockSpec((1, MLSTM_V_W), lambda b, c: (0, 0)),
        ],
        out_specs=pl.BlockSpec((blk, MLSTM_V_W), idx(0)),
        out_shape=jax.ShapeDtypeStruct((rows, MLSTM_V_W), BF16),
        scratch_shapes=[
            pltpu.VMEM((MLSTM_HEADS, MLSTM_DK, MLSTM_DV), F32),
            pltpu.VMEM((MLSTM_HEADS, 1, MLSTM_DK), F32),
            pltpu.VMEM((MLSTM_HEADS, 1, LANES), F32),
        ],
        compiler_params=_params("parallel", "arbitrary"),
        name="mlstm_core",
    )(p_main, p_main, p_main, p_main, gates, norm_w.reshape(1, MLSTM_V_W))


def _pool_kernel(h_ref, halo_ref, nw_ref, w_ref, sc_ref, o_ref, ext_ref, *, tm, rows_per_batch):
    i = pl.program_id(0)
    seq = rows_per_batch - SEQ_BLOCK
    x = h_ref[...]
    nw = nw_ref[...]
    u = _rms_normalize(x) * nw
    ext_ref[0:POOL_HALO, :] = _rms_normalize(halo_ref[...]) * nw
    ext_ref[POOL_HALO:, :] = u
    row = lax.rem(i * tm, rows_per_batch) + lax.broadcasted_iota(jnp.int32, (tm, 1), 0)
    pos = jnp.where(row < seq, row + N_META, row - (seq + LEAD_PAD))
    valid = pos >= 0
    for gi, win in enumerate(POOL_WINDOWS):
        sl = slice(gi * POOL_GROUP, (gi + 1) * POOL_GROUP)
        terms = [ext_ref[POOL_HALO - s:POOL_HALO - s + tm, sl] for s in range(win)]
        while len(terms) > 1:
            terms = [terms[a] + terms[a + 1] for a in range(0, len(terms), 2)]
        count = jnp.clip(pos + 1, 1, win).astype(F32)
        pooled = terms[0] / count - u[:, sl]
        y = _dot(pooled.astype(BF16), w_ref[gi].astype(BF16)) * sc_ref[:, sl]
        o_ref[:, sl] = x[:, sl] + jnp.where(valid, y, 0.0)


def _pool_mixer(h, nw, w_group, scale, rows_per_batch, *, tm_target=384):
    rows, d = h.shape
    tm = _divisor_tile(rows_per_batch, tm_target, POOL_HALO)
    tpb = rows_per_batch // tm
    halo_per_tile = tm // POOL_HALO
    halo_per_batch = rows_per_batch // POOL_HALO

    def halo_index(i):
        return jnp.where(i % tpb == 0, (i // tpb + 1) * halo_per_batch - 1, i * halo_per_tile - 1), 0

    kern = functools.partial(_pool_kernel, tm=tm, rows_per_batch=rows_per_batch)
    return pl.pallas_call(
        kern,
        grid=(rows // tm,),
        in_specs=[
            pl.BlockSpec((tm, d), lambda i: (i, 0)),
            pl.BlockSpec((POOL_HALO, d), halo_index),
            pl.BlockSpec((1, d), lambda i: (0, 0)),
            pl.BlockSpec(w_group.shape, lambda i: (0, 0, 0)),
            pl.BlockSpec((1, d), lambda i: (0, 0)),
        ],
        out_specs=pl.BlockSpec((tm, d), lambda i: (i, 0)),
        out_shape=jax.ShapeDtypeStruct((rows, d), F32),
        scratch_shapes=[pltpu.VMEM((tm + POOL_HALO, d), F32)],
        compiler_params=_params("parallel"),
        name="pool_mixer",
    )(h, h, nw.reshape(1, d), w_group, scale.reshape(1, d))


GDN_CARRY = 16


def _unit_lower_inverses(strict_lowers):
    n = strict_lowers[0].shape[0]
    ri = lax.broadcasted_iota(jnp.int32, (n, n), 0)
    ci = lax.broadcasted_iota(jnp.int32, (n, n), 1)
    ident = jnp.where(ri == ci, 1.0, 0.0)
    invs = [ident - l for l in strict_lowers]
    powers_b = [l.astype(BF16) for l in strict_lowers]
    terms = 2
    while terms < n:
        powers_b = [_dot(ab, ab).astype(BF16) for ab in powers_b]
        invs = [p + _dot(p.astype(BF16), ab) for p, ab in zip(invs, powers_b)]
        terms *= 2
    return invs


def _gdn_chunk(r0, gc_all, beta_all, s_ref, q_s, k_s, v_s, o_s, head_group):
    ch = GDN_CHUNK
    rows = slice(r0, r0 + ch)
    ri = lax.broadcasted_iota(jnp.int32, (ch, ch), 0)
    ci = lax.broadcasted_iota(jnp.int32, (ch, ch), 1)
    eye = ri == ci
    causal = ci <= ri
    strict = ci < ri
    rep = GDN_V_HEADS // GDN_QK_HEADS
    for g0 in range(0, GDN_V_HEADS, head_group):
        vheads = range(g0, g0 + head_group)
        qheads = range(g0 // rep, (g0 + head_group) // rep)
        q = {j: q_s[j, rows, :] for j in qheads}
        k = {j: k_s[j, rows, :] for j in qheads}
        kk = {j: _dot_nt(k[j], k[j]) for j in qheads}
        qk = {j: _dot_nt(q[j], k[j]) for j in qheads}
        gc_c = {jv: gc_all[:, jv:jv + 1] for jv in vheads}
        beta_c = {jv: beta_all[:, jv:jv + 1] for jv in vheads}
        decay = {jv: jnp.exp(jnp.where(causal, gc_c[jv] - _column_to_row(gc_c[jv], eye), -jnp.inf)) for jv in vheads}
        t_inv = _unit_lower_inverses([jnp.where(strict, kk[jv // rep] * decay[jv], 0.0) * beta_c[jv] for jv in vheads])
        t_inv = dict(zip(vheads, t_inv))
        egc = {jv: jnp.exp(gc_c[jv]) for jv in vheads}
        s_st = {jv: s_ref[jv] for jv in vheads}
        s_b = {jv: s_st[jv].astype(BF16) for jv in vheads}
        qs = {jv: _dot(q[jv // rep], s_b[jv]) * egc[jv] for jv in vheads}
        rhs = {jv: jnp.concatenate([v_s[jv, rows, :].astype(F32) * beta_c[jv],
                                    k[jv // rep].astype(F32) * (beta_c[jv] * egc[jv])], axis=-1).astype(BF16)
               for jv in vheads}
        sol = {jv: _dot(t_inv[jv].astype(BF16), rhs[jv]) for jv in vheads}
        v_new = {jv: sol[jv][:, :GDN_DV] - _dot(sol[jv][:, GDN_DV:].astype(BF16), s_b[jv]) for jv in vheads}
        for jv in vheads:
            o_s[jv, rows, :] = qs[jv] + _dot((qk[jv // rep] * decay[jv]).astype(BF16), v_new[jv].astype(BF16))
        for jv in vheads:
            g_last = gc_c[jv][ch - 1:ch, :]
            v_dec = (v_new[jv] * jnp.exp(g_last - gc_c[jv])).astype(BF16)
            s_ref[jv] = jnp.exp(g_last) * s_st[jv] + _dot_tn(k[jv // rep], v_dec)


def _gdn_kernel(x_ref, z_ref, e_ref, cw_ref, al_ref, dt_ref, nw_ref, o_ref,
                ext_ref, s_ref, q_s, k_s, v_s, o_s, *, head_group):
    c = pl.program_id(1)
    blk = SEQ_BLOCK

    @pl.when(c == 0)
    def _():
        ext_ref[blk:, :] = jnp.zeros((GDN_CARRY, GDN_CONV_CH), BF16)
        s_ref[...] = jnp.zeros_like(s_ref)

    ext_ref[0:blk, :] = x_ref[...]
    n_shift = GDN_CONV - 1
    r = lax.broadcasted_iota(jnp.int32, (n_shift * blk, blk + GDN_CARRY), 0)
    src_col = lax.broadcasted_iota(jnp.int32, (n_shift * blk, blk + GDN_CARRY), 1)
    shift = 1 + (r >= blk).astype(jnp.int32) + (r >= 2 * blk).astype(jnp.int32)
    t = r - (shift - 1) * blk
    src = jnp.where(t >= shift, t - shift, blk + GDN_CARRY + t - shift)
    select_rows = jnp.where(src_col == src, 1.0, 0.0).astype(BF16)
    for cb2 in range(GDN_CONV_CH // (2 * LANES)):
        shifted = _dot(select_rows, ext_ref[:, cb2 * 2 * LANES:(cb2 + 1) * 2 * LANES])
        for half in range(2):
            cb = 2 * cb2 + half
            sl = slice(cb * LANES, (cb + 1) * LANES)
            acc = x_ref[:, sl].astype(F32) * cw_ref[n_shift:n_shift + 1, sl]
            for s in range(1, GDN_CONV):
                x_back = shifted[(s - 1) * blk:s * blk, half * LANES:(half + 1) * LANES]
                acc = acc + x_back * cw_ref[n_shift - s:n_shift - s + 1, sl]
            y = acc * _sigmoid(acc)
            if cb < 2 * GDN_QK_HEADS:
                y = y * lax.rsqrt(jnp.sum(y * y, axis=-1, keepdims=True) + EPS)
                if cb < GDN_QK_HEADS:
                    q_s[cb] = (y * (GDN_DK ** -0.5)).astype(BF16)
                else:
                    k_s[cb - GDN_QK_HEADS] = y.astype(BF16)
            else:
                v_s[cb - 2 * GDN_QK_HEADS] = y.astype(BF16)
    ext_ref[blk:, :] = x_ref[blk - GDN_CARRY:, :]

    row = c * blk + lax.broadcasted_iota(jnp.int32, (blk, 1), 0)
    valid = row >= LEAD_PAD
    e = e_ref[...]
    beta_all = jnp.where(valid, _sigmoid(e[:, :GDN_V_HEADS]), 0.0)
    g_all = jnp.where(valid, -jnp.exp(al_ref[...]) * _softplus(e[:, GDN_V_HEADS:] + dt_ref[...]), 0.0)

    for r0 in range(0, blk, GDN_CHUNK):
        rows = slice(r0, r0 + GDN_CHUNK)
        _gdn_chunk(r0, _cumsum_rows(g_all[rows]), beta_all[rows], s_ref, q_s, k_s, v_s, o_s, head_group)

    for jv in range(GDN_V_HEADS):
        sl = slice(jv * GDN_DV, (jv + 1) * GDN_DV)
        z = z_ref[:, sl].astype(F32)
        o_ref[:, sl] = (_rms_normalize(o_s[jv]) * nw_ref[...] * (z * _sigmoid(z))).astype(o_ref.dtype)


def _gdn_core(p_main, side, conv_w, a_log, dt_bias, norm_w, batch, rows_per_batch, *, head_group=32):
    blk = SEQ_BLOCK
    nblk = rows_per_batch // blk
    rows = p_main.shape[0]
    idx = lambda col: (lambda b, c: (_physical_block(b, c, nblk), col))
    const2 = lambda b, c: (0, 0)
    kern = functools.partial(_gdn_kernel, head_group=head_group)
    return pl.pallas_call(
        kern,
        grid=(batch, nblk),
        in_specs=[
            pl.BlockSpec((blk, GDN_CONV_CH), idx(0)),
            pl.BlockSpec((blk, GDN_V_W), idx(GDN_CONV_CH // GDN_V_W)),
            pl.BlockSpec((blk, 2 * GDN_V_HEADS), idx(0)),
            pl.BlockSpec((GDN_CONV, GDN_CONV_CH), const2),
            pl.BlockSpec((1, GDN_V_HEADS), const2),
            pl.BlockSpec((1, GDN_V_HEADS), const2),
            pl.BlockSpec((1, GDN_DV), const2),
        ],
        out_specs=pl.BlockSpec((blk, GDN_V_W), idx(0)),
        out_shape=jax.ShapeDtypeStruct((rows, GDN_V_W), BF16),
        scratch_shapes=[
            pltpu.VMEM((blk + GDN_CARRY, GDN_CONV_CH), BF16),
            pltpu.VMEM((GDN_V_HEADS, GDN_DK, GDN_DV), F32),
            pltpu.VMEM((GDN_QK_HEADS, blk, GDN_DK), BF16),
            pltpu.VMEM((GDN_QK_HEADS, blk, GDN_DK), BF16),
            pltpu.VMEM((GDN_V_HEADS, blk, GDN_DV), BF16),
            pltpu.VMEM((GDN_V_HEADS, blk, GDN_DV), F32),
        ],
        compiler_params=_params("parallel", "arbitrary"),
        name="gdn_core",
    )(p_main, p_main, side, conv_w, a_log.reshape(1, -1), dt_bias.reshape(1, -1), norm_w.reshape(1, -1))


def _rope_table_kernel(cos_ref, sin_ref):
    shape = cos_ref.shape
    pos = (lax.broadcasted_iota(jnp.int32, shape, 0) - LEAD_PAD).astype(F32)
    pair = lax.rem(lax.broadcasted_iota(jnp.int32, shape, 1), SWA_DH // 2).astype(F32)
    inv = jnp.exp(pair * (-2.0 * math.log(ROPE_THETA) / SWA_DH))
    ang = pos * inv
    cos_ref[...] = jnp.cos(ang)
    sin_ref[...] = jnp.sin(ang)


def _rope_tables(rows_per_batch):
    shape = jax.ShapeDtypeStruct((rows_per_batch, LANES), F32)
    return pl.pallas_call(_rope_table_kernel, out_shape=[shape, shape], name="rope_tables")()


SWA_COL_BATCH = 4


def _swa_kernel(sink_ref, q_ref, kv_ref, kvp_ref, cos_ref, sin_ref, cosp_ref, sinp_ref, o_ref):
    n = pl.program_id(1)
    blk = SEQ_BLOCK
    lane = lax.broadcasted_iota(jnp.int32, (1, LANES), 1)
    low = lane < SWA_DH
    first_half = lax.rem(lane, SWA_DH) < SWA_DH // 2

    def rope(x, cos, sin):
        rot = jnp.where(first_half, -pltpu.roll(x, LANES - SWA_DH // 2, 1), pltpu.roll(x, SWA_DH // 2, 1))
        return x * cos + rot * sin

    cos, sin = cos_ref[...], sin_ref[...]
    cosp, sinp = cosp_ref[...], sinp_ref[...]

    stacked = lax.broadcasted_iota(jnp.int32, (2 * blk, 2 * blk), 0)
    q_row = n * blk + jnp.where(stacked < blk, stacked, stacked - blk)
    k_row = (n - 1) * blk + lax.broadcasted_iota(jnp.int32, (2 * blk, 2 * blk), 1)
    mask = (k_row <= q_row) & (q_row - k_row < SWA_WINDOW) & (k_row >= LEAD_PAD)
    top = lax.broadcasted_iota(jnp.int32, (2 * blk, 1), 0) < blk

    def both_heads(x, head_in_pair):
        swapped = pltpu.roll(x, SWA_DH, 1)
        return jnp.where(low, x, swapped) if head_in_pair == 0 else jnp.where(low, swapped, x)

    kk, v_stack = {}, {}
    for g in range(SWA_HKV):
        pair_col, head_in_pair = divmod(g, 2)
        ksl = slice(pair_col * LANES, (pair_col + 1) * LANES)
        vsl = slice(SWA_KV_W + pair_col * LANES, SWA_KV_W + (pair_col + 1) * LANES)
        k_cur = both_heads(rope(kv_ref[:, ksl].astype(F32), cos, sin), head_in_pair)
        k_prev = both_heads(rope(kvp_ref[:, ksl].astype(F32), cosp, sinp), head_in_pair)
        kk[g] = jnp.concatenate([k_prev, k_cur], axis=0).astype(BF16)
        v_cur = both_heads(kv_ref[:, vsl].astype(F32), head_in_pair)
        v_prev = both_heads(kvp_ref[:, vsl].astype(F32), head_in_pair)
        v2 = jnp.concatenate([v_prev, v_cur], axis=0)
        v_stack[g] = jnp.concatenate([jnp.where(low, v2, 0.0), jnp.where(low, 0.0, v2)], axis=0).astype(BF16)

    def scores(col):
        qp = rope(q_ref[:, col * LANES:(col + 1) * LANES].astype(F32), cos, sin)
        q2 = jnp.concatenate([jnp.where(low, qp, 0.0), jnp.where(low, 0.0, qp)], axis=0).astype(BF16)
        s = _dot_nt(q2, kk[col // (SWA_GROUP // 2)]) * (SWA_DH ** -0.5)
        return jnp.where(mask, s, -jnp.inf)

    def attend(cols, s):
        sink = {c: jnp.where(top, sink_ref[2 * c], sink_ref[2 * c + 1]) for c in cols}
        m = {c: jnp.maximum(jnp.max(s[c], axis=-1, keepdims=True), sink[c]) for c in cols}
        e = {c: jnp.exp(s[c] - m[c]) for c in cols}
        inv = {c: 1.0 / (jnp.sum(e[c], axis=-1, keepdims=True) + jnp.exp(sink[c] - m[c])) for c in cols}
        for c in cols:
            prob = e[c] * inv[c]
            p2 = jnp.concatenate([prob[:blk], prob[blk:]], axis=1).astype(BF16)
            o_ref[:, c * LANES:(c + 1) * LANES] = _dot(p2, v_stack[c // (SWA_GROUP // 2)]).astype(o_ref.dtype)

    n_cols = SWA_HQ // 2
    batches = [range(c0, c0 + SWA_COL_BATCH) for c0 in range(0, n_cols, SWA_COL_BATCH)]
    pending = {c: scores(c) for c in batches[0]}
    for bi, cols in enumerate(batches):
        current = {c: pending.pop(c) for c in cols}
        if bi + 1 < len(batches):
            pending.update({c: scores(c) for c in batches[bi + 1]})
        attend(cols, current)


def _swa_core(p_main, sinks, cos, sin, batch, rows_per_batch):
    blk = SEQ_BLOCK
    nblk = rows_per_batch // blk
    rows = p_main.shape[0]
    kv_col = SWA_Q_W // (2 * SWA_KV_W)
    cur = lambda col: (lambda b, n: (_physical_block(b, n, nblk), col))
    prev = lambda col: (lambda b, n: (_physical_block(b, jnp.maximum(n - 1, 0), nblk), col))
    tab_cur = lambda b, n: (n, 0)
    tab_prev = lambda b, n: (jnp.maximum(n - 1, 0), 0)
    return pl.pallas_call(
        _swa_kernel,
        grid=(batch, nblk),
        in_specs=[
            pl.BlockSpec(memory_space=pltpu.SMEM),
            pl.BlockSpec((blk, SWA_Q_W), cur(0)),
            pl.BlockSpec((blk, 2 * SWA_KV_W), cur(kv_col)),
            pl.BlockSpec((blk, 2 * SWA_KV_W), prev(kv_col)),
            pl.BlockSpec((blk, LANES), tab_cur),
            pl.BlockSpec((blk, LANES), tab_cur),
            pl.BlockSpec((blk, LANES), tab_prev),
            pl.BlockSpec((blk, LANES), tab_prev),
        ],
        out_specs=pl.BlockSpec((blk, SWA_Q_W), cur(0)),
        out_shape=jax.ShapeDtypeStruct((rows, SWA_Q_W), BF16),
        compiler_params=_params("parallel", "parallel"),
        name="swa_core",
    )(sinks, p_main, p_main, p_main, cos, sin, cos, sin)


def kernel(x, meta_tokens, norm_w, ffn_w_gate, ffn_w_up, ffn_w_down, mlstm_w_in, mlstm_b_if, mlstm_norm_w, mlstm_w_out, pool_w, pool_scale, gdn_w_in, gdn_conv_w, gdn_a_log, gdn_dt_bias, gdn_norm_w, gdn_w_out, swa_w_qkv, swa_b_qkv, swa_sinks, swa_w_out, swa_b_out, final_norm_w):
    batch, seq, d = x.shape
    rpb = _rows_per_batch(seq)
    zeros_d = jnp.zeros((d,), F32)
    h = x
    for i in range(DEPTH):
        m, j = i % N_MIXERS, i // N_MIXERS
        h = _ffn(h, norm_w[i, 0], ffn_w_gate, ffn_w_up, ffn_w_down, i, 0, rpb,
                 meta=meta_tokens.astype(x.dtype) if i == 0 else None)
        h = h.reshape(batch * rpb, d)
        if m == 0:
            p_main, gates = _norm_proj(h, norm_w[i, 1], mlstm_w_in[j].T, jnp.zeros((MLSTM_MAIN_W,), F32),
                                       MLSTM_MAIN_W, mlstm_b_if[j], w_transposed=True)
            a = _mlstm_core(p_main, gates, mlstm_norm_w[j], batch, rpb)
            h = _proj_residual(a, mlstm_w_out[j], zeros_d, h, rpb)
        elif m == 1:
            h = _pool_mixer(h, norm_w[i, 1], pool_w[j], pool_scale[j], rpb)
        elif m == 2:
            p_main, side = _norm_proj(h, norm_w[i, 1], gdn_w_in[j].T, jnp.zeros((GDN_MAIN_W,), F32), GDN_MAIN_W,
                                      jnp.zeros((2 * GDN_V_HEADS,), F32), w_transposed=True)
            a = _gdn_core(p_main, side, gdn_conv_w[j], gdn_a_log[j], gdn_dt_bias[j], gdn_norm_w[j], batch, rpb)
            h = _proj_residual(a, gdn_w_out[j], zeros_d, h, rpb)
        else:
            (p_main,) = _norm_proj(h, norm_w[i, 1], swa_w_qkv[j], swa_b_qkv[j], SWA_IN)
            cos, sin = _rope_tables(rpb)
            a = _swa_core(p_main, swa_sinks[j], cos, sin, batch, rpb)
            h = _proj_residual(a, swa_w_out[j], swa_b_out[j], h, rpb)
        h = _ffn(h.reshape(batch, rpb, d), norm_w[i, 2], ffn_w_gate, ffn_w_up, ffn_w_down, i, 1, rpb,
                 final_w=final_norm_w if i == DEPTH - 1 else None)
    return h
```

```python
import functools
import math

import jax
import jax.numpy as jnp
from jax import lax
from jax.experimental import pallas as pl
from jax.experimental.pallas import tpu as pltpu

F32 = jnp.float32
BF16 = jnp.bfloat16

D_MODEL = 2048
DEPTH = 4
N_MIXERS = 4
N_META = 16
EPS = 1e-6

MLSTM_HEADS = 8
MLSTM_DV = 256
MLSTM_DK = 128
MLSTM_QK_W = MLSTM_HEADS * MLSTM_DK
MLSTM_V_W = MLSTM_HEADS * MLSTM_DV
MLSTM_MAIN_W = 2 * MLSTM_QK_W + 2 * MLSTM_V_W

POOL_WINDOWS = (2, 4, 8, 16)
POOL_GROUP = D_MODEL // len(POOL_WINDOWS)
POOL_HALO = 16

GDN_DK = 128
GDN_DV = 128
GDN_QK_HEADS = 16
GDN_V_HEADS = 32
GDN_CONV = 4
GDN_CHUNK = 64
GDN_QK_W = GDN_QK_HEADS * GDN_DK
GDN_V_W = GDN_V_HEADS * GDN_DV
GDN_CONV_CH = 2 * GDN_QK_W + GDN_V_W
GDN_MAIN_W = GDN_CONV_CH + GDN_V_W

SWA_DH = 64
SWA_HQ = 32
SWA_GROUP = 8
SWA_HKV = 4
SWA_WINDOW = 128
SWA_Q_W = SWA_HQ * SWA_DH
SWA_KV_W = SWA_HKV * SWA_DH
SWA_IN = SWA_Q_W + 2 * SWA_KV_W
ROPE_THETA = 10000.0

LANES = 128
SEQ_BLOCK = 128
LEAD_PAD = SEQ_BLOCK - N_META

VMEM_LIMIT_BYTES = 60 * 1024 * 1024
WEIGHT_SLOTS = 3
FFN_PROLOGUE_ROWS = 384


def _rows_per_batch(seq):
    assert seq % SEQ_BLOCK == 0, seq
    return seq + SEQ_BLOCK


def _divisor_tile(total, target, multiple):
    best = None
    for t in range(multiple, min(total, target) + 1, multiple):
        if total % t == 0:
            best = t
    assert best is not None, (total, target, multiple)
    return best


def _params(*semantics):
    return pltpu.CompilerParams(dimension_semantics=semantics, vmem_limit_bytes=VMEM_LIMIT_BYTES)


def _physical_block(b, n, nblk):
    return b * nblk + jnp.where(n == 0, nblk - 1, n - 1)


def _is_token_row(row_in_batch, seq):
    return (row_in_batch < seq) | (row_in_batch >= seq + LEAD_PAD)


def _rms_normalize(x):
    return x * lax.rsqrt(jnp.mean(x * x, axis=-1, keepdims=True) + EPS)


def _sigmoid(x):
    return 1.0 / (1.0 + jnp.exp(-x))


def _softplus(x):
    return jnp.maximum(x, 0.0) + jnp.log(1.0 + jnp.exp(-jnp.abs(x)))


def _dot(a, b):
    return jnp.dot(a, b, preferred_element_type=F32)


def _dot_nt(a, b):
    return lax.dot_general(a, b, (((1,), (1,)), ((), ())), preferred_element_type=F32)


def _dot_tn(a, b):
    return lax.dot_general(a, b, (((0,), (0,)), ((), ())), preferred_element_type=F32)


def _cumsum_rows(x):
    n = x.shape[0]
    row = lax.broadcasted_iota(jnp.int32, x.shape, 0)
    shift = 1
    while shift < n:
        x = x + jnp.where(row >= shift, pltpu.roll(x, shift, 0), 0.0)
        shift *= 2
    return x


def _column_to_row(col, eye):
    return jnp.sum(jnp.where(eye, col, 0.0), axis=0, keepdims=True)


def _advance_weight_ring(copies, step, n_steps):
    ahead = WEIGHT_SLOTS - 1

    @pl.when(step == 0)
    def _():
        for k in range(ahead):
            for cp in copies(k, k):
                cp.start()

    @pl.when(step + ahead < n_steps)
    def _():
        for cp in copies(step + ahead, lax.rem(step + ahead, WEIGHT_SLOTS)):
            cp.start()

    slot = lax.rem(step, WEIGHT_SLOTS)
    for cp in copies(step, slot):
        cp.wait()
    return slot


def _ffn_kernel(*refs, assemble, final_norm, tiles_per_batch, layer, half):
    h_ref, nw_ref, wg_hbm, wu_hbm, wd_hbm = refs[:5]
    rest = list(refs[5:])
    meta_ref = rest.pop(0) if assemble else None
    fw_ref = rest.pop(0) if final_norm else None
    o_ref, xn_ref, wg_buf, wu_buf, wd_buf, w_sem = rest
    j = pl.program_id(1)
    n_j = pl.num_programs(1)

    tm = o_ref.shape[0]
    tf = wg_buf.shape[2]
    sub = _divisor_tile(tm, FFN_PROLOGUE_ROWS, 16)
    assert sub >= SEQ_BLOCK

    def weight_copies(step, slot):
        col = pl.multiple_of(lax.rem(step, n_j) * tf, tf)
        return (
            pltpu.make_async_copy(wg_hbm.at[layer, half, :, pl.ds(col, tf)], wg_buf.at[slot], w_sem.at[0, slot]),
            pltpu.make_async_copy(wu_hbm.at[layer, half, :, pl.ds(col, tf)], wu_buf.at[slot], w_sem.at[1, slot]),
            pltpu.make_async_copy(wd_hbm.at[layer, half, pl.ds(col, tf), :], wd_buf.at[slot], w_sem.at[2, slot]),
        )

    slot = _advance_weight_ring(weight_copies, pl.program_id(0) * n_j + j, pl.num_programs(0) * n_j)
    wg_ref, wu_ref, wd_ref = wg_buf.at[slot], wu_buf.at[slot], wd_buf.at[slot]

    def swiglu_rows(r0, r1, wg, wu, wd):
        xn = xn_ref[r0:r1, :]
        g = _dot(xn, wg)
        u = _dot(xn, wu)
        a = (0.5 * g * _sigmoid(g) * u).astype(BF16)
        o_ref[r0:r1, :] += _dot(a, wd)

    @pl.when(j == 0)
    def _():
        body_end = tm
        if assemble:
            body_end = tm - SEQ_BLOCK
            ends_batch = lax.rem(pl.program_id(0), tiles_per_batch) == tiles_per_batch - 1

            @pl.when(ends_batch)
            def _():
                o_ref[body_end:tm - N_META, :] = jnp.zeros((LEAD_PAD, o_ref.shape[1]), F32)
                o_ref[tm - N_META:, :] = meta_ref[...]

            @pl.when(jnp.logical_not(ends_batch))
            def _():
                o_ref[body_end:, :] = h_ref[body_end:, :]

        wg, wu, wd = (w[...].astype(BF16) for w in (wg_ref, wu_ref, wd_ref))
        for r0 in range(0, tm, sub):
            r1 = r0 + sub
            rc = min(r1, body_end)
            if rc > r0:
                o_ref[r0:rc, :] = h_ref[r0:rc, :]
            xn_ref[r0:r1, :] = (_rms_normalize(o_ref[r0:r1, :]) * nw_ref[...]).astype(BF16)
            swiglu_rows(r0, r1, wg, wu, wd)

    @pl.when(j > 0)
    def _():
        swiglu_rows(0, tm, *(w[...].astype(BF16) for w in (wg_ref, wu_ref, wd_ref)))

    if final_norm:
        @pl.when(j == pl.num_programs(1) - 1)
        def _():
            o_ref[...] = _rms_normalize(o_ref[...]) * fw_ref[...]


def _ffn(h, nw, w_gate, w_up, w_down, layer, half, rows_per_batch, *, meta=None, final_w=None,
         tm_target=1056, tf=256):
    batch, _, d = h.shape
    d_ff = w_gate.shape[-1]
    tm = _divisor_tile(rows_per_batch, tm_target, SEQ_BLOCK // 8)
    assert tm >= SEQ_BLOCK and d_ff % tf == 0
    tpb = rows_per_batch // tm
    assert batch * tpb * (d_ff // tf) >= WEIGHT_SLOTS
    row_spec = pl.BlockSpec((None, tm, d), lambda i, j: (i // tpb, i % tpb, 0))
    vec_spec = pl.BlockSpec((1, d), lambda i, j: (0, 0))
    hbm_spec = pl.BlockSpec(memory_space=pl.ANY)
    in_specs = [row_spec, vec_spec, hbm_spec, hbm_spec, hbm_spec]
    args = [h, nw.reshape(1, d), w_gate, w_up, w_down]
    if meta is not None:
        in_specs.append(pl.BlockSpec((N_META, d), lambda i, j: (0, 0)))
        args.append(meta)
    if final_w is not None:
        in_specs.append(vec_spec)
        args.append(final_w.reshape(1, d))
    out_rows = rows_per_batch - SEQ_BLOCK if final_w is not None else rows_per_batch
    kern = functools.partial(_ffn_kernel, assemble=meta is not None, final_norm=final_w is not None,
                             tiles_per_batch=tpb, layer=layer, half=half)
    return pl.pallas_call(
        kern,
        grid=(batch * tpb, d_ff // tf),
        in_specs=in_specs,
        out_specs=row_spec,
        out_shape=jax.ShapeDtypeStruct((batch, out_rows, d), F32),
        scratch_shapes=[
            pltpu.VMEM((tm, d), BF16),
            pltpu.VMEM((WEIGHT_SLOTS, d, tf), w_gate.dtype),
            pltpu.VMEM((WEIGHT_SLOTS, d, tf), w_up.dtype),
            pltpu.VMEM((WEIGHT_SLOTS, tf, d), w_down.dtype),
            pltpu.SemaphoreType.DMA((3, WEIGHT_SLOTS)),
        ],
        compiler_params=_params("arbitrary", "arbitrary"),
        name="ffn",
    )(*args)


def _norm_proj_kernel(*refs, has_side, w_transposed):
    if has_side:
        h_ref, nw_ref, w_hbm, b_ref, ws_ref, bs_ref, o_ref, os_ref, xn_ref, w_buf, w_sem = refs
    else:
        h_ref, nw_ref, w_hbm, b_ref, o_ref, xn_ref, w_buf, w_sem = refs
    j = pl.program_id(1)
    n_j = pl.num_programs(1)
    tm, tn = o_ref.shape
    sub = _divisor_tile(tm, FFN_PROLOGUE_ROWS, 16)
    matmul = _dot_nt if w_transposed else _dot

    def weight_copies(step, slot):
        col = pl.multiple_of(lax.rem(step, n_j) * tn, tn)
        src = w_hbm.at[pl.ds(col, tn), :] if w_transposed else w_hbm.at[:, pl.ds(col, tn)]
        return (pltpu.make_async_copy(src, w_buf.at[slot], w_sem.at[slot]),)

    w_ref = w_buf.at[_advance_weight_ring(weight_copies, pl.program_id(0) * n_j + j, pl.num_programs(0) * n_j)]

    @pl.when(j == 0)
    def _():
        w = w_ref[...].astype(BF16)
        if has_side:
            n_side = os_ref.shape[1]
            ws = (ws_ref[:n_side, :] if w_transposed else ws_ref[:, :n_side]).astype(BF16)
        for r0 in range(0, tm, sub):
            rows = slice(r0, r0 + sub)
            xn = (_rms_normalize(h_ref[rows, :]) * nw_ref[...]).astype(BF16)
            xn_ref[rows, :] = xn
            o_ref[rows, :] = (matmul(xn, w) + b_ref[...]).astype(o_ref.dtype)
            if has_side:
                os_ref[rows, :] = matmul(xn, ws) + bs_ref[...]

    @pl.when(j > 0)
    def _():
        o_ref[...] = (matmul(xn_ref[...], w_ref[...].astype(BF16)) + b_ref[...]).astype(o_ref.dtype)


def _norm_proj(h, nw, w, bias, n_main, b_side=None, *, w_transposed=False, tm_target=1056, tn_target=1024):
    rows, d = h.shape
    n_total = w.shape[0] if w_transposed else w.shape[1]
    tm = _divisor_tile(rows, tm_target, 16)
    tn = _divisor_tile(n_main, tn_target, 2 * LANES)
    w_block = (lambda width, col: pl.BlockSpec((width, d), lambda i, j: (col(j), 0))) if w_transposed else \
              (lambda width, col: pl.BlockSpec((d, width), lambda i, j: (0, col(j))))
    assert (rows // tm) * (n_main // tn) >= WEIGHT_SLOTS
    in_specs = [
        pl.BlockSpec((tm, d), lambda i, j: (i, 0)),
        pl.BlockSpec((1, d), lambda i, j: (0, 0)),
        pl.BlockSpec(memory_space=pl.ANY),
        pl.BlockSpec((1, tn), lambda i, j: (0, j)),
    ]
    out_specs = [pl.BlockSpec((tm, tn), lambda i, j: (i, j))]
    out_shape = [jax.ShapeDtypeStruct((rows, n_main), BF16)]
    args = [h, nw.reshape(1, d), w, bias.reshape(1, -1)]
    if b_side is not None:
        n_side = n_total - n_main
        assert n_main % LANES == 0 and 0 < n_side <= LANES and b_side.shape == (n_side,)
        in_specs += [w_block(LANES, lambda j: n_main // LANES), pl.BlockSpec((1, n_side), lambda i, j: (0, 0))]
        out_specs += [pl.BlockSpec((tm, n_side), lambda i, j: (i, 0))]
        out_shape += [jax.ShapeDtypeStruct((rows, n_side), F32)]
        args += [w, b_side.reshape(1, n_side)]
    return pl.pallas_call(
        functools.partial(_norm_proj_kernel, has_side=b_side is not None, w_transposed=w_transposed),
        grid=(rows // tm, n_main // tn),
        in_specs=in_specs,
        out_specs=out_specs,
        out_shape=out_shape,
        scratch_shapes=[
            pltpu.VMEM((tm, d), BF16),
            pltpu.VMEM((WEIGHT_SLOTS, tn, d) if w_transposed else (WEIGHT_SLOTS, d, tn), w.dtype),
            pltpu.SemaphoreType.DMA((WEIGHT_SLOTS,)),
        ],
        compiler_params=_params("arbitrary", "arbitrary"),
        name="norm_proj",
    )(*args)


def _proj_residual_kernel(a_ref, w_ref, b_ref, h_ref, o_ref, wb_ref, *, tm, rows_per_batch):
    i = pl.program_id(1)

    @pl.when(i == 0)
    def _():
        wb_ref[...] = w_ref[...].astype(BF16)

    y = _dot(a_ref[...], wb_ref[...]) + b_ref[...]
    row = lax.rem(i * tm, rows_per_batch) + lax.broadcasted_iota(jnp.int32, (tm, 1), 0)
    o_ref[...] = h_ref[...] + jnp.where(_is_token_row(row, rows_per_batch - SEQ_BLOCK), y, 0.0)


PROJ_WEIGHT_TILE_BYTES = 16 * 1024 * 1024


def _proj_residual(a, w, bias, h, rows_per_batch, *, tm_target=528):
    rows, k = a.shape
    d = h.shape[1]
    tm = _divisor_tile(rows_per_batch, tm_target, 16)
    tn = _divisor_tile(d, PROJ_WEIGHT_TILE_BYTES // (4 * k), 2 * LANES)
    kern = functools.partial(_proj_residual_kernel, tm=tm, rows_per_batch=rows_per_batch)
    return pl.pallas_call(
        kern,
        grid=(d // tn, rows // tm),
        in_specs=[
            pl.BlockSpec((tm, k), lambda j, i: (i, 0)),
            pl.BlockSpec((k, tn), lambda j, i: (0, j), pipeline_mode=pl.Buffered(1)),
            pl.BlockSpec((1, tn), lambda j, i: (0, j)),
            pl.BlockSpec((tm, tn), lambda j, i: (i, j)),
        ],
        out_specs=pl.BlockSpec((tm, tn), lambda j, i: (i, j)),
        out_shape=jax.ShapeDtypeStruct((rows, d), F32),
        scratch_shapes=[pltpu.VMEM((k, tn), BF16)],
        compiler_params=_params("parallel", "arbitrary"),
        name="proj_residual",
    )(a, w, bias.reshape(1, d), h)


def _mlstm_kernel(q_ref, k_ref, v_ref, og_ref, g_ref, nw_ref, o_ref, c_ref, n_ref, m_ref):
    c = pl.program_id(1)
    blk = SEQ_BLOCK
    heads = range(MLSTM_HEADS)

    @pl.when(c == 0)
    def _():
        c_ref[...] = jnp.zeros_like(c_ref)
        n_ref[...] = jnp.zeros_like(n_ref)
        m_ref[...] = jnp.zeros_like(m_ref)

    gates = g_ref[...]
    row = c * blk + lax.broadcasted_iota(jnp.int32, (blk, 1), 0)
    valid = row >= LEAD_PAD
    log_f = jnp.where(valid, jnp.minimum(gates, 0.0) - jnp.log(1.0 + jnp.exp(-jnp.abs(gates))), 0.0)
    b_all = _cumsum_rows(log_f)
    li_all = jnp.where(valid, gates, -jnp.inf)

    ri = lax.broadcasted_iota(jnp.int32, (blk, blk), 0)
    ci = lax.broadcasted_iota(jnp.int32, (blk, blk), 1)
    eye = ri == ci
    causal = ci <= ri
    scale = MLSTM_DK ** -0.5

    q = {hd: q_ref[:, hd * MLSTM_DK:(hd + 1) * MLSTM_DK] for hd in heads}
    k = {hd: k_ref[:, hd * MLSTM_DK:(hd + 1) * MLSTM_DK] for hd in heads}
    v = {hd: v_ref[:, hd * MLSTM_DV:(hd + 1) * MLSTM_DV] for hd in heads}
    c_st = {hd: c_ref[hd] for hd in heads}
    n_st = {hd: n_ref[hd] for hd in heads}
    m_prev = {hd: m_ref[hd][:, :1] for hd in heads}

    qk_raw = {hd: _dot_nt(q[hd], k[hd]) for hd in heads}
    q_c = {hd: _dot(q[hd], c_st[hd].astype(BF16)) for hd in heads}

    b_c = {hd: b_all[:, MLSTM_HEADS + hd:MLSTM_HEADS + hd + 1] for hd in heads}
    d_c = {hd: li_all[:, hd:hd + 1] - b_c[hd] for hd in heads}
    b_last = {hd: b_c[hd][blk - 1:blk, :] for hd in heads}

    a_init, ka = {}, {}
    for hd in heads:
        log_end_init = b_last[hd] + m_prev[hd]
        log_end = b_last[hd] + d_c[hd]
        m_new = jnp.maximum(log_end_init, jnp.max(log_end, axis=0, keepdims=True))
        a_init[hd] = jnp.exp(log_end_init - m_new)
        ka[hd] = k[hd].astype(F32) * jnp.exp(log_end - m_new)
        m_ref[hd] = jnp.broadcast_to(m_new, (1, LANES))
    kv = {hd: _dot_tn(ka[hd].astype(BF16), v[hd]) for hd in heads}

    d_r = {hd: _column_to_row(d_c[hd], eye) for hd in heads}
    log_w = {hd: jnp.where(causal, b_c[hd] + d_r[hd], -jnp.inf) for hd in heads}
    log_init = {hd: b_c[hd] + m_prev[hd] for hd in heads}
    m_t = {hd: jnp.maximum(log_init[hd], jnp.max(log_w[hd], axis=-1, keepdims=True)) for hd in heads}
    w_init = {hd: jnp.exp(log_init[hd] - m_t[hd]) * scale for hd in heads}
    qk = {hd: qk_raw[hd] * (jnp.exp(log_w[hd] - m_t[hd]) * scale) for hd in heads}
    pv = {hd: _dot(qk[hd].astype(BF16), v[hd]) for hd in heads}
    qn = {hd: jnp.sum(q[hd].astype(F32) * n_st[hd], axis=-1, keepdims=True) for hd in heads}
    den = {hd: w_init[hd] * qn[hd] + jnp.sum(qk[hd], axis=-1, keepdims=True) for hd in heads}
    for hd in heads:
        hh = (w_init[hd] * q_c[hd] + pv[hd]) / jnp.maximum(jnp.abs(den[hd]), jnp.exp(-m_t[hd]))
        sl = slice(hd * MLSTM_DV, (hd + 1) * MLSTM_DV)
        gate = _sigmoid(og_ref[:, sl].astype(F32))
        o_ref[:, sl] = (_rms_normalize(hh) * nw_ref[:, sl] * gate).astype(o_ref.dtype)

    for hd in heads:
        c_ref[hd] = a_init[hd] * c_st[hd] + kv[hd]
        n_ref[hd] = a_init[hd] * n_st[hd] + jnp.sum(ka[hd], axis=0, keepdims=True)


def _mlstm_core(p_main, gates, norm_w, batch, rows_per_batch):
    blk = SEQ_BLOCK
    nblk = rows_per_batch // blk
    rows = p_main.shape[0]
    idx = lambda col: (lambda b, c: (_physical_block(b, c, nblk), col))
    return pl.pallas_call(
        _mlstm_kernel,
        grid=(batch, nblk),
        in_specs=[
            pl.BlockSpec((blk, MLSTM_QK_W), idx(0)),
            pl.BlockSpec((blk, MLSTM_QK_W), idx(1)),
            pl.BlockSpec((blk, MLSTM_V_W), idx(1)),
            pl.BlockSpec((blk, MLSTM_V_W), idx(2)),
            pl.BlockSpec((blk, 2 * MLSTM_HEADS), idx(0)),
            pl.BlockSpec((1, MLSTM_V_W), lambda b, c: (0, 0)),
        ],
        out_specs=pl.BlockSpec((blk, MLSTM_V_W), idx(0)),
        out_shape=jax.ShapeDtypeStruct((rows, MLSTM_V_W), BF16),
        scratch_shapes=[
            pltpu.VMEM((MLSTM_HEADS, MLSTM_DK, MLSTM_DV), F32),
            pltpu.VMEM((MLSTM_HEADS, 1, MLSTM_DK), F32),
            pltpu.VMEM((MLSTM_HEADS, 1, LANES), F32),
        ],
        compiler_params=_params("parallel", "arbitrary"),
        name="mlstm_core",
    )(p_main, p_main, p_main, p_main, gates, norm_w.reshape(1, MLSTM_V_W))


def _pool_kernel(h_ref, halo_ref, nw_ref, w_ref, sc_ref, o_ref, ext_ref, *, tm, rows_per_batch):
    i = pl.program_id(0)
    seq = rows_per_batch - SEQ_BLOCK
    x = h_ref[...]
    nw = nw_ref[...]
    u = _rms_normalize(x) * nw
    ext_ref[0:POOL_HALO, :] = _rms_normalize(halo_ref[...]) * nw
    ext_ref[POOL_HALO:, :] = u
    row = lax.rem(i * tm, rows_per_batch) + lax.broadcasted_iota(jnp.int32, (tm, 1), 0)
    pos = jnp.where(row < seq, row + N_META, row - (seq + LEAD_PAD))
    valid = pos >= 0
    for gi, win in enumerate(POOL_WINDOWS):
        sl = slice(gi * POOL_GROUP, (gi + 1) * POOL_GROUP)
        terms = [ext_ref[POOL_HALO - s:POOL_HALO - s + tm, sl] for s in range(win)]
        while len(terms) > 1:
            terms = [terms[a] + terms[a + 1] for a in range(0, len(terms), 2)]
        count = jnp.clip(pos + 1, 1, win).astype(F32)
        pooled = terms[0] / count - u[:, sl]
        y = _dot(pooled.astype(BF16), w_ref[gi].astype(BF16)) * sc_ref[:, sl]
        o_ref[:, sl] = x[:, sl] + jnp.where(valid, y, 0.0)


def _pool_mixer(h, nw, w_group, scale, rows_per_batch, *, tm_target=384):
    rows, d = h.shape
    tm = _divisor_tile(rows_per_batch, tm_target, POOL_HALO)
    tpb = rows_per_batch // tm
    halo_per_tile = tm // POOL_HALO
    halo_per_batch = rows_per_batch // POOL_HALO

    def halo_index(i):
        return jnp.where(i % tpb == 0, (i // tpb + 1) * halo_per_batch - 1, i * halo_per_tile - 1), 0

    kern = functools.partial(_pool_kernel, tm=tm, rows_per_batch=rows_per_batch)
    return pl.pallas_call(
        kern,
        grid=(rows // tm,),
        in_specs=[
            pl.BlockSpec((tm, d), lambda i: (i, 0)),
            pl.BlockSpec((POOL_HALO, d), halo_index),
            pl.BlockSpec((1, d), lambda i: (0, 0)),
            pl.BlockSpec(w_group.shape, lambda i: (0, 0, 0)),
            pl.BlockSpec((1, d), lambda i: (0, 0)),
        ],
        out_specs=pl.BlockSpec((tm, d), lambda i: (i, 0)),
        out_shape=jax.ShapeDtypeStruct((rows, d), F32),
        scratch_shapes=[pltpu.VMEM((tm + POOL_HALO, d), F32)],
        compiler_params=_params("parallel"),
        name="pool_mixer",
    )(h, h, nw.reshape(1, d), w_group, scale.reshape(1, d))


GDN_CARRY = 16


def _unit_lower_inverses(strict_lowers):
    n = strict_lowers[0].shape[0]
    ri = lax.broadcasted_iota(jnp.int32, (n, n), 0)
    ci = lax.broadcasted_iota(jnp.int32, (n, n), 1)
    ident = jnp.where(ri == ci, 1.0, 0.0)
    invs = [ident - l for l in strict_lowers]
    powers_b = [l.astype(BF16) for l in strict_lowers]
    terms = 2
    while terms < n:
        powers_b = [_dot(ab, ab).astype(BF16) for ab in powers_b]
        invs = [p + _dot(p.astype(BF16), ab) for p, ab in zip(invs, powers_b)]
        terms *= 2
    return invs


def _gdn_chunk(r0, gc_all, beta_all, s_ref, q_s, k_s, v_s, o_s, head_group):
    ch = GDN_CHUNK
    rows = slice(r0, r0 + ch)
    ri = lax.broadcasted_iota(jnp.int32, (ch, ch), 0)
    ci = lax.broadcasted_iota(jnp.int32, (ch, ch), 1)
    eye = ri == ci
    causal = ci <= ri
    strict = ci < ri
    rep = GDN_V_HEADS // GDN_QK_HEADS
    for g0 in range(0, GDN_V_HEADS, head_group):
        vheads = range(g0, g0 + head_group)
        qheads = range(g0 // rep, (g0 + head_group) // rep)
        q = {j: q_s[j, rows, :] for j in qheads}
        k = {j: k_s[j, rows, :] for j in qheads}
        kk = {j: _dot_nt(k[j], k[j]) for j in qheads}
        qk = {j: _dot_nt(q[j], k[j]) for j in qheads}
        gc_c = {jv: gc_all[:, jv:jv + 1] for jv in vheads}
        beta_c = {jv: beta_all[:, jv:jv + 1] for jv in vheads}
        decay = {jv: jnp.exp(jnp.where(causal, gc_c[jv] - _column_to_row(gc_c[jv], eye), -jnp.inf)) for jv in vheads}
        t_inv = _unit_lower_inverses([jnp.where(strict, kk[jv // rep] * decay[jv], 0.0) * beta_c[jv] for jv in vheads])
        t_inv = dict(zip(vheads, t_inv))
        egc = {jv: jnp.exp(gc_c[jv]) for jv in vheads}
        s_st = {jv: s_ref[jv] for jv in vheads}
        s_b = {jv: s_st[jv].astype(BF16) for jv in vheads}
        qs = {jv: _dot(q[jv // rep], s_b[jv]) * egc[jv] for jv in vheads}
        rhs = {jv: jnp.concatenate([v_s[jv, rows, :].astype(F32) * beta_c[jv],
                                    k[jv // rep].astype(F32) * (beta_c[jv] * egc[jv])], axis=-1).astype(BF16)
               for jv in vheads}
        sol = {jv: _dot(t_inv[jv].astype(BF16), rhs[jv]) for jv in vheads}
        v_new = {jv: sol[jv][:, :GDN_DV] - _dot(sol[jv][:, GDN_DV:].astype(BF16), s_b[jv]) for jv in vheads}
        for jv in vheads:
            o_s[jv, rows, :] = qs[jv] + _dot((qk[jv // rep] * decay[jv]).astype(BF16), v_new[jv].astype(BF16))
        for jv in vheads:
            g_last = gc_c[jv][ch - 1:ch, :]
            v_dec = (v_new[jv] * jnp.exp(g_last - gc_c[jv])).astype(BF16)
            s_ref[jv] = jnp.exp(g_last) * s_st[jv] + _dot_tn(k[jv // rep], v_dec)


def _gdn_kernel(x_ref, z_ref, e_ref, cw_ref, al_ref, dt_ref, nw_ref, o_ref,
                ext_ref, s_ref, q_s, k_s, v_s, o_s, *, head_group):
    c = pl.program_id(1)
    blk = SEQ_BLOCK

    @pl.when(c == 0)
    def _():
        ext_ref[blk:, :] = jnp.zeros((GDN_CARRY, GDN_CONV_CH), BF16)
        s_ref[...] = jnp.zeros_like(s_ref)

    ext_ref[0:blk, :] = x_ref[...]
    n_shift = GDN_CONV - 1
    r = lax.broadcasted_iota(jnp.int32, (n_shift * blk, blk + GDN_CARRY), 0)
    src_col = lax.broadcasted_iota(jnp.int32, (n_shift * blk, blk + GDN_CARRY), 1)
    shift = 1 + (r >= blk).astype(jnp.int32) + (r >= 2 * blk).astype(jnp.int32)
    t = r - (shift - 1) * blk
    src = jnp.where(t >= shift, t - shift, blk + GDN_CARRY + t - shift)
    select_rows = jnp.where(src_col == src, 1.0, 0.0).astype(BF16)
    for cb2 in range(GDN_CONV_CH // (2 * LANES)):
        shifted = _dot(select_rows, ext_ref[:, cb2 * 2 * LANES:(cb2 + 1) * 2 * LANES])
        for half in range(2):
            cb = 2 * cb2 + half
            sl = slice(cb * LANES, (cb + 1) * LANES)
            acc = x_ref[:, sl].astype(F32) * cw_ref[n_shift:n_shift + 1, sl]
            for s in range(1, GDN_CONV):
                x_back = shifted[(s - 1) * blk:s * blk, half * LANES:(half + 1) * LANES]
                acc = acc + x_back * cw_ref[n_shift - s:n_shift - s + 1, sl]
            y = acc * _sigmoid(acc)
            if cb < 2 * GDN_QK_HEADS:
                y = y * lax.rsqrt(jnp.sum(y * y, axis=-1, keepdims=True) + EPS)
                if cb < GDN_QK_HEADS:
                    q_s[cb] = (y * (GDN_DK ** -0.5)).astype(BF16)
                else:
                    k_s[cb - GDN_QK_HEADS] = y.astype(BF16)
            else:
                v_s[cb - 2 * GDN_QK_HEADS] = y.astype(BF16)
    ext_ref[blk:, :] = x_ref[blk - GDN_CARRY:, :]

    row = c * blk + lax.broadcasted_iota(jnp.int32, (blk, 1), 0)
    valid = row >= LEAD_PAD
    e = e_ref[...]
    beta_all = jnp.where(valid, _sigmoid(e[:, :GDN_V_HEADS]), 0.0)
    g_all = jnp.where(valid, -jnp.exp(al_ref[...]) * _softplus(e[:, GDN_V_HEADS:] + dt_ref[...]), 0.0)

    for r0 in range(0, blk, GDN_CHUNK):
        rows = slice(r0, r0 + GDN_CHUNK)
        _gdn_chunk(r0, _cumsum_rows(g_all[rows]), beta_all[rows], s_ref, q_s, k_s, v_s, o_s, head_group)

    for jv in range(GDN_V_HEADS):
        sl = slice(jv * GDN_DV, (jv + 1) * GDN_DV)
        z = z_ref[:, sl].astype(F32)
        o_ref[:, sl] = (_rms_normalize(o_s[jv]) * nw_ref[...] * (z * _sigmoid(z))).astype(o_ref.dtype)


def _gdn_core(p_main, side, conv_w, a_log, dt_bias, norm_w, batch, rows_per_batch, *, head_group=32):
    blk = SEQ_BLOCK
    nblk = rows_per_batch // blk
    rows = p_main.shape[0]
    idx = lambda col: (lambda b, c: (_physical_block(b, c, nblk), col))
    const2 = lambda b, c: (0, 0)
    kern = functools.partial(_gdn_kernel, head_group=head_group)
    return pl.pallas_call(
        kern,
        grid=(batch, nblk),
        in_specs=[
            pl.BlockSpec((blk, GDN_CONV_CH), idx(0)),
            pl.BlockSpec((blk, GDN_V_W), idx(GDN_CONV_CH // GDN_V_W)),
            pl.BlockSpec((blk, 2 * GDN_V_HEADS), idx(0)),
            pl.BlockSpec((GDN_CONV, GDN_CONV_CH), const2),
            pl.BlockSpec((1, GDN_V_HEADS), const2),
            pl.BlockSpec((1, GDN_V_HEADS), const2),
            pl.BlockSpec((1, GDN_DV), const2),
        ],
        out_specs=pl.BlockSpec((blk, GDN_V_W), idx(0)),
        out_shape=jax.ShapeDtypeStruct((rows, GDN_V_W), BF16),
        scratch_shapes=[
            pltpu.VMEM((blk + GDN_CARRY, GDN_CONV_CH), BF16),
            pltpu.VMEM((GDN_V_HEADS, GDN_DK, GDN_DV), F32),
            pltpu.VMEM((GDN_QK_HEADS, blk, GDN_DK), BF16),
            pltpu.VMEM((GDN_QK_HEADS, blk, GDN_DK), BF16),
            pltpu.VMEM((GDN_V_HEADS, blk, GDN_DV), BF16),
            pltpu.VMEM((GDN_V_HEADS, blk, GDN_DV), F32),
        ],
        compiler_params=_params("parallel", "arbitrary"),
        name="gdn_core",
    )(p_main, p_main, side, conv_w, a_log.reshape(1, -1), dt_bias.reshape(1, -1), norm_w.reshape(1, -1))


def _rope_table_kernel(cos_ref, sin_ref):
    shape = cos_ref.shape
    pos = (lax.broadcasted_iota(jnp.int32, shape, 0) - LEAD_PAD).astype(F32)
    pair = lax.rem(lax.broadcasted_iota(jnp.int32, shape, 1), SWA_DH // 2).astype(F32)
    inv = jnp.exp(pair * (-2.0 * math.log(ROPE_THETA) / SWA_DH))
    ang = pos * inv
    cos_ref[...] = jnp.cos(ang)
    sin_ref[...] = jnp.sin(ang)


def _rope_tables(rows_per_batch):
    shape = jax.ShapeDtypeStruct((rows_per_batch, LANES), F32)
    return pl.pallas_call(_rope_table_kernel, out_shape=[shape, shape], name="rope_tables")()


SWA_COL_BATCH = 4


def _swa_kernel(sink_ref, q_ref, kv_ref, kvp_ref, cos_ref, sin_ref, cosp_ref, sinp_ref, o_ref):
    n = pl.program_id(1)
    blk = SEQ_BLOCK
    lane = lax.broadcasted_iota(jnp.int32, (1, LANES), 1)
    low = lane < SWA_DH
    first_half = lax.rem(lane, SWA_DH) < SWA_DH // 2

    def rope(x, cos, sin):
        rot = jnp.where(first_half, -pltpu.roll(x, LANES - SWA_DH // 2, 1), pltpu.roll(x, SWA_DH // 2, 1))
        return x * cos + rot * sin

    cos, sin = cos_ref[...], sin_ref[...]
    cosp, sinp = cosp_ref[...], sinp_ref[...]

    stacked = lax.broadcasted_iota(jnp.int32, (2 * blk, 2 * blk), 0)
    q_row = n * blk + jnp.where(stacked < blk, stacked, stacked - blk)
    k_row = (n - 1) * blk + lax.broadcasted_iota(jnp.int32, (2 * blk, 2 * blk), 1)
    mask = (k_row <= q_row) & (q_row - k_row < SWA_WINDOW) & (k_row >= LEAD_PAD)
    top = lax.broadcasted_iota(jnp.int32, (2 * blk, 1), 0) < blk

    def both_heads(x, head_in_pair):
        swapped = pltpu.roll(x, SWA_DH, 1)
        return jnp.where(low, x, swapped) if head_in_pair == 0 else jnp.where(low, swapped, x)

    kk, v_stack = {}, {}
    for g in range(SWA_HKV):
        pair_col, head_in_pair = divmod(g, 2)
        ksl = slice(pair_col * LANES, (pair_col + 1) * LANES)
        vsl = slice(SWA_KV_W + pair_col * LANES, SWA_KV_W + (pair_col + 1) * LANES)
        k_cur = both_heads(rope(kv_ref[:, ksl].astype(F32), cos, sin), head_in_pair)
        k_prev = both_heads(rope(kvp_ref[:, ksl].astype(F32), cosp, sinp), head_in_pair)
        kk[g] = jnp.concatenate([k_prev, k_cur], axis=0).astype(BF16)
        v_cur = both_heads(kv_ref[:, vsl].astype(F32), head_in_pair)
        v_prev = both_heads(kvp_ref[:, vsl].astype(F32), head_in_pair)
        v2 = jnp.concatenate([v_prev, v_cur], axis=0)
        v_stack[g] = jnp.concatenate([jnp.where(low, v2, 0.0), jnp.where(low, 0.0, v2)], axis=0).astype(BF16)

    def scores(col):
        qp = rope(q_ref[:, col * LANES:(col + 1) * LANES].astype(F32), cos, sin)
        q2 = jnp.concatenate([jnp.where(low, qp, 0.0), jnp.where(low, 0.0, qp)], axis=0).astype(BF16)
        s = _dot_nt(q2, kk[col // (SWA_GROUP // 2)]) * (SWA_DH ** -0.5)
        return jnp.where(mask, s, -jnp.inf)

    def attend(cols, s):
        sink = {c: jnp.where(top, sink_ref[2 * c], sink_ref[2 * c + 1]) for c in cols}
        m = {c: jnp.maximum(jnp.max(s[c], axis=-1, keepdims=True), sink[c]) for c in cols}
        e = {c: jnp.exp(s[c] - m[c]) for c in cols}
        inv = {c: 1.0 / (jnp.sum(e[c], axis=-1, keepdims=True) + jnp.exp(sink[c] - m[c])) for c in cols}
        for c in cols:
            prob = e[c] * inv[c]
            p2 = jnp.concatenate([prob[:blk], prob[blk:]], axis=1).astype(BF16)
            o_ref[:, c * LANES:(c + 1) * LANES] = _dot(p2, v_stack[c // (SWA_GROUP // 2)]).astype(o_ref.dtype)

    n_cols = SWA_HQ // 2
    batches = [range(c0, c0 + SWA_COL_BATCH) for c0 in range(0, n_cols, SWA_COL_BATCH)]
    pending = {c: scores(c) for c in batches[0]}
    for bi, cols in enumerate(batches):
        current = {c: pending.pop(c) for c in cols}
        if bi + 1 < len(batches):
            pending.update({c: scores(c) for c in batches[bi + 1]})
        attend(cols, current)


def _swa_core(p_main, sinks, cos, sin, batch, rows_per_batch):
    blk = SEQ_BLOCK
    nblk = rows_per_batch // blk
    rows = p_main.shape[0]
    kv_col = SWA_Q_W // (2 * SWA_KV_W)
    cur = lambda col: (lambda b, n: (_physical_block(b, n, nblk), col))
    prev = lambda col: (lambda b, n: (_physical_block(b, jnp.maximum(n - 1, 0), nblk), col))
    tab_cur = lambda b, n: (n, 0)
    tab_prev = lambda b, n: (jnp.maximum(n - 1, 0), 0)
    return pl.pallas_call(
        _swa_kernel,
        grid=(batch, nblk),
        in_specs=[
            pl.BlockSpec(memory_space=pltpu.SMEM),
            pl.BlockSpec((blk, SWA_Q_W), cur(0)),
            pl.BlockSpec((blk, 2 * SWA_KV_W), cur(kv_col)),
            pl.BlockSpec((blk, 2 * SWA_KV_W), prev(kv_col)),
            pl.BlockSpec((blk, LANES), tab_cur),
            pl.BlockSpec((blk, LANES), tab_cur),
            pl.BlockSpec((blk, LANES), tab_prev),
            pl.BlockSpec((blk, LANES), tab_prev),
        ],
        out_specs=pl.BlockSpec((blk, SWA_Q_W), cur(0)),
        out_shape=jax.ShapeDtypeStruct((rows, SWA_Q_W), BF16),
        compiler_params=_params("parallel", "parallel"),
        name="swa_core",
    )(sinks, p_main, p_main, p_main, cos, sin, cos, sin)


def kernel(x, meta_tokens, norm_w, ffn_w_gate, ffn_w_up, ffn_w_down, mlstm_w_in, mlstm_b_if, mlstm_norm_w, mlstm_w_out, pool_w, pool_scale, gdn_w_in, gdn_conv_w, gdn_a_log, gdn_dt_bias, gdn_norm_w, gdn_w_out, swa_w_qkv, swa_b_qkv, swa_sinks, swa_w_out, swa_b_out, final_norm_w):
    batch, seq, d = x.shape
    rpb = _rows_per_batch(seq)
    zeros_d = jnp.zeros((d,), F32)
    h = x
    for i in range(DEPTH):
        m, j = i % N_MIXERS, i // N_MIXERS
        h = _ffn(h, norm_w[i, 0], ffn_w_gate, ffn_w_up, ffn_w_down, i, 0, rpb,
                 meta=meta_tokens.astype(x.dtype) if i == 0 else None)
        h = h.reshape(batch * rpb, d)
        if m == 0:
            p_main, gates = _norm_proj(h, norm_w[i, 1], mlstm_w_in[j].T, jnp.zeros((MLSTM_MAIN_W,), F32),
                                       MLSTM_MAIN_W, mlstm_b_if[j], w_transposed=True)
            a = _mlstm_core(p_main, gates, mlstm_norm_w[j], batch, rpb)
            h = _proj_residual(a, mlstm_w_out[j], zeros_d, h, rpb)
        elif m == 1:
            h = _pool_mixer(h, norm_w[i, 1], pool_w[j], pool_scale[j], rpb)
        elif m == 2:
            p_main, side = _norm_proj(h, norm_w[i, 1], gdn_w_in[j].T, jnp.zeros((GDN_MAIN_W,), F32), GDN_MAIN_W,
                                      jnp.zeros((2 * GDN_V_HEADS,), F32), w_transposed=True)
            a = _gdn_core(p_main, side, gdn_conv_w[j], gdn_a_log[j], gdn_dt_bias[j], gdn_norm_w[j], batch, rpb)
            h = _proj_residual(a, gdn_w_out[j], zeros_d, h, rpb)
        else:
            (p_main,) = _norm_proj(h, norm_w[i, 1], swa_w_qkv[j], swa_b_qkv[j], SWA_IN)
            cos, sin = _rope_tables(rpb)
            a = _swa_core(p_main, swa_sinks[j], cos, sin, batch, rpb)
            h = _proj_residual(a, swa_w_out[j], swa_b_out[j], h, rpb)
        h = _ffn(h.reshape(batch, rpb, d), norm_w[i, 2], ffn_w_gate, ffn_w_up, ffn_w_down, i, 1, rpb,
                 final_w=final_norm_w if i == DEPTH - 1 else None)
    return h
```

```python
import functools
import math

import jax
import jax.numpy as jnp
from jax import lax
from jax.experimental import pallas as pl
from jax.experimental.pallas import tpu as pltpu

F32 = jnp.float32
BF16 = jnp.bfloat16

D_MODEL = 2048
DEPTH = 4
N_MIXERS = 4
N_META = 16
EPS = 1e-6

MLSTM_HEADS = 8
MLSTM_DV = 256
MLSTM_DK = 128
MLSTM_QK_W = MLSTM_HEADS * MLSTM_DK
MLSTM_V_W = MLSTM_HEADS * MLSTM_DV
MLSTM_MAIN_W = 2 * MLSTM_QK_W + 2 * MLSTM_V_W

POOL_WINDOWS = (2, 4, 8, 16)
POOL_GROUP = D_MODEL // len(POOL_WINDOWS)
POOL_HALO = 16

GDN_DK = 128
GDN_DV = 128
GDN_QK_HEADS = 16
GDN_V_HEADS = 32
GDN_CONV = 4
GDN_CHUNK = 64
GDN_QK_W = GDN_QK_HEADS * GDN_DK
GDN_V_W = GDN_V_HEADS * GDN_DV
GDN_CONV_CH = 2 * GDN_QK_W + GDN_V_W
GDN_MAIN_W = GDN_CONV_CH + GDN_V_W

SWA_DH = 64
SWA_HQ = 32
SWA_GROUP = 8
SWA_HKV = 4
SWA_WINDOW = 128
SWA_Q_W = SWA_HQ * SWA_DH
SWA_KV_W = SWA_HKV * SWA_DH
SWA_IN = SWA_Q_W + 2 * SWA_KV_W
ROPE_THETA = 10000.0

LANES = 128
SEQ_BLOCK = 128
LEAD_PAD = SEQ_BLOCK - N_META

VMEM_LIMIT_BYTES = 60 * 1024 * 1024
WEIGHT_SLOTS = 3
FFN_PROLOGUE_ROWS = 384


def _rows_per_batch(seq):
    assert seq % SEQ_BLOCK == 0, seq
    return seq + SEQ_BLOCK


def _divisor_tile(total, target, multiple):
    best = None
    for t in range(multiple, min(total, target) + 1, multiple):
        if total % t == 0:
            best = t
    assert best is not None, (total, target, multiple)
    return best


def _params(*semantics):
    return pltpu.CompilerParams(dimension_semantics=semantics, vmem_limit_bytes=VMEM_LIMIT_BYTES)


def _physical_block(b, n, nblk):
    return b * nblk + jnp.where(n == 0, nblk - 1, n - 1)


def _is_token_row(row_in_batch, seq):
    return (row_in_batch < seq) | (row_in_batch >= seq + LEAD_PAD)


def _rms_normalize(x):
    return x * lax.rsqrt(jnp.mean(x * x, axis=-1, keepdims=True) + EPS)


def _sigmoid(x):
    return 1.0 / (1.0 + jnp.exp(-x))


def _softplus(x):
    return jnp.maximum(x, 0.0) + jnp.log(1.0 + jnp.exp(-jnp.abs(x)))


def _dot(a, b):
    return jnp.dot(a, b, preferred_element_type=F32)


def _dot_nt(a, b):
    return lax.dot_general(a, b, (((1,), (1,)), ((), ())), preferred_element_type=F32)


def _dot_tn(a, b):
    return lax.dot_general(a, b, (((0,), (0,)), ((), ())), preferred_element_type=F32)


def _cumsum_rows(x):
    n = x.shape[0]
    row = lax.broadcasted_iota(jnp.int32, x.shape, 0)
    shift = 1
    while shift < n:
        x = x + jnp.where(row >= shift, pltpu.roll(x, shift, 0), 0.0)
        shift *= 2
    return x


def _column_to_row(col, eye):
    return jnp.sum(jnp.where(eye, col, 0.0), axis=0, keepdims=True)


def _advance_weight_ring(copies, step, n_steps):
    ahead = WEIGHT_SLOTS - 1

    @pl.when(step == 0)
    def _():
        for k in range(ahead):
            for cp in copies(k, k):
                cp.start()

    @pl.when(step + ahead < n_steps)
    def _():
        for cp in copies(step + ahead, lax.rem(step + ahead, WEIGHT_SLOTS)):
            cp.start()

    slot = lax.rem(step, WEIGHT_SLOTS)
    for cp in copies(step, slot):
        cp.wait()
    return slot


def _ffn_kernel(*refs, assemble, final_norm, tiles_per_batch, layer, half):
    h_ref, nw_ref, wg_hbm, wu_hbm, wd_hbm = refs[:5]
    rest = list(refs[5:])
    meta_ref = rest.pop(0) if assemble else None
    fw_ref = rest.pop(0) if final_norm else None
    o_ref, xn_ref, wg_buf, wu_buf, wd_buf, w_sem = rest
    j = pl.program_id(1)
    n_j = pl.num_programs(1)

    tm = o_ref.shape[0]
    tf = wg_buf.shape[2]
    sub = _divisor_tile(tm, FFN_PROLOGUE_ROWS, 16)
    assert sub >= SEQ_BLOCK

    def weight_copies(step, slot):
        col = pl.multiple_of(lax.rem(step, n_j) * tf, tf)
        return (
            pltpu.make_async_copy(wg_hbm.at[layer, half, :, pl.ds(col, tf)], wg_buf.at[slot], w_sem.at[0, slot]),
            pltpu.make_async_copy(wu_hbm.at[layer, half, :, pl.ds(col, tf)], wu_buf.at[slot], w_sem.at[1, slot]),
            pltpu.make_async_copy(wd_hbm.at[layer, half, pl.ds(col, tf), :], wd_buf.at[slot], w_sem.at[2, slot]),
        )

    slot = _advance_weight_ring(weight_copies, pl.program_id(0) * n_j + j, pl.num_programs(0) * n_j)
    wg_ref, wu_ref, wd_ref = wg_buf.at[slot], wu_buf.at[slot], wd_buf.at[slot]

    def swiglu_rows(r0, r1, wg, wu, wd):
        xn = xn_ref[r0:r1, :]
        g = _dot(xn, wg)
        u = _dot(xn, wu)
        a = (0.5 * g * _sigmoid(g) * u).astype(BF16)
        o_ref[r0:r1, :] += _dot(a, wd)

    @pl.when(j == 0)
    def _():
        body_end = tm
        if assemble:
            body_end = tm - SEQ_BLOCK
            ends_batch = lax.rem(pl.program_id(0), tiles_per_batch) == tiles_per_batch - 1

            @pl.when(ends_batch)
            def _():
                o_ref[body_end:tm - N_META, :] = jnp.zeros((LEAD_PAD, o_ref.shape[1]), F32)
                o_ref[tm - N_META:, :] = meta_ref[...]

            @pl.when(jnp.logical_not(ends_batch))
            def _():
                o_ref[body_end:, :] = h_ref[body_end:, :]

        wg, wu, wd = (w[...].astype(BF16) for w in (wg_ref, wu_ref, wd_ref))
        for r0 in range(0, tm, sub):
            r1 = r0 + sub
            rc = min(r1, body_end)
            if rc > r0:
                o_ref[r0:rc, :] = h_ref[r0:rc, :]
            xn_ref[r0:r1, :] = (_rms_normalize(o_ref[r0:r1, :]) * nw_ref[...]).astype(BF16)
            swiglu_rows(r0, r1, wg, wu, wd)

    @pl.when(j > 0)
    def _():
        swiglu_rows(0, tm, *(w[...].astype(BF16) for w in (wg_ref, wu_ref, wd_ref)))

    if final_norm:
        @pl.when(j == pl.num_programs(1) - 1)
        def _():
            o_ref[...] = _rms_normalize(o_ref[...]) * fw_ref[...]


def _ffn(h, nw, w_gate, w_up, w_down, layer, half, rows_per_batch, *, meta=None, final_w=None,
         tm_target=1056, tf=256):
    batch, _, d = h.shape
    d_ff = w_gate.shape[-1]
    tm = _divisor_tile(rows_per_batch, tm_target, SEQ_BLOCK // 8)
    assert tm >= SEQ_BLOCK and d_ff % tf == 0
    tpb = rows_per_batch // tm
    assert batch * tpb * (d_ff // tf) >= WEIGHT_SLOTS
    row_spec = pl.BlockSpec((None, tm, d), lambda i, j: (i // tpb, i % tpb, 0))
    vec_spec = pl.BlockSpec((1, d), lambda i, j: (0, 0))
    hbm_spec = pl.BlockSpec(memory_space=pl.ANY)
    in_specs = [row_spec, vec_spec, hbm_spec, hbm_spec, hbm_spec]
    args = [h, nw.reshape(1, d), w_gate, w_up, w_down]
    if meta is not None:
        in_specs.append(pl.BlockSpec((N_META, d), lambda i, j: (0, 0)))
        args.append(meta)
    if final_w is not None:
        in_specs.append(vec_spec)
        args.append(final_w.reshape(1, d))
    out_rows = rows_per_batch - SEQ_BLOCK if final_w is not None else rows_per_batch
    kern = functools.partial(_ffn_kernel, assemble=meta is not None, final_norm=final_w is not None,
                             tiles_per_batch=tpb, layer=layer, half=half)
    return pl.pallas_call(
        kern,
        grid=(batch * tpb, d_ff // tf),
        in_specs=in_specs,
        out_specs=row_spec,
        out_shape=jax.ShapeDtypeStruct((batch, out_rows, d), F32),
        scratch_shapes=[
            pltpu.VMEM((tm, d), BF16),
            pltpu.VMEM((WEIGHT_SLOTS, d, tf), w_gate.dtype),
            pltpu.VMEM((WEIGHT_SLOTS, d, tf), w_up.dtype),
            pltpu.VMEM((WEIGHT_SLOTS, tf, d), w_down.dtype),
            pltpu.SemaphoreType.DMA((3, WEIGHT_SLOTS)),
        ],
        compiler_params=_params("arbitrary", "arbitrary"),
        name="ffn",
    )(*args)


def _norm_proj_kernel(*refs, has_side, w_transposed):
    if has_side:
        h_ref, nw_ref, w_hbm, b_ref, ws_ref, bs_ref, o_ref, os_ref, xn_ref, w_buf, w_sem = refs
    else:
        h_ref, nw_ref, w_hbm, b_ref, o_ref, xn_ref, w_buf, w_sem = refs
    j = pl.program_id(1)
    n_j = pl.num_programs(1)
    tm, tn = o_ref.shape
    sub = _divisor_tile(tm, FFN_PROLOGUE_ROWS, 16)
    matmul = _dot_nt if w_transposed else _dot

    def weight_copies(step, slot):
        col = pl.multiple_of(lax.rem(step, n_j) * tn, tn)
        src = w_hbm.at[pl.ds(col, tn), :] if w_transposed else w_hbm.at[:, pl.ds(col, tn)]
        return (pltpu.make_async_copy(src, w_buf.at[slot], w_sem.at[slot]),)

    w_ref = w_buf.at[_advance_weight_ring(weight_copies, pl.program_id(0) * n_j + j, pl.num_programs(0) * n_j)]

    @pl.when(j == 0)
    def _():
        w = w_ref[...].astype(BF16)
        if has_side:
            n_side = os_ref.shape[1]
            ws = (ws_ref[:n_side, :] if w_transposed else ws_ref[:, :n_side]).astype(BF16)
        for r0 in range(0, tm, sub):
            rows = slice(r0, r0 + sub)
            xn = (_rms_normalize(h_ref[rows, :]) * nw_ref[...]).astype(BF16)
            xn_ref[rows, :] = xn
            o_ref[rows, :] = (matmul(xn, w) + b_ref[...]).astype(o_ref.dtype)
            if has_side:
                os_ref[rows, :] = matmul(xn, ws) + bs_ref[...]

    @pl.when(j > 0)
    def _():
        o_ref[...] = (matmul(xn_ref[...], w_ref[...].astype(BF16)) + b_ref[...]).astype(o_ref.dtype)


def _norm_proj(h, nw, w, bias, n_main, b_side=None, *, w_transposed=False, tm_target=1056, tn_target=1024):
    rows, d = h.shape
    n_total = w.shape[0] if w_transposed else w.shape[1]
    tm = _divisor_tile(rows, tm_target, 16)
    tn = _divisor_tile(n_main, tn_target, 2 * LANES)
    w_block = (lambda width, col: pl.BlockSpec((width, d), lambda i, j: (col(j), 0))) if w_transposed else \
              (lambda width, col: pl.BlockSpec((d, width), lambda i, j: (0, col(j))))
    assert (rows // tm) * (n_main // tn) >= WEIGHT_SLOTS
    in_specs = [
        pl.BlockSpec((tm, d), lambda i, j: (i, 0)),
        pl.BlockSpec((1, d), lambda i, j: (0, 0)),
        pl.BlockSpec(memory_space=pl.ANY),
        pl.BlockSpec((1, tn), lambda i, j: (0, j)),
    ]
    out_specs = [pl.BlockSpec((tm, tn), lambda i, j: (i, j))]
    out_shape = [jax.ShapeDtypeStruct((rows, n_main), BF16)]
    args = [h, nw.reshape(1, d), w, bias.reshape(1, -1)]
    if b_side is not None:
        n_side = n_total - n_main
        assert n_main % LANES == 0 and 0 < n_side <= LANES and b_side.shape == (n_side,)
        in_specs += [w_block(LANES, lambda j: n_main // LANES), pl.BlockSpec((1, n_side), lambda i, j: (0, 0))]
        out_specs += [pl.BlockSpec((tm, n_side), lambda i, j: (i, 0))]
        out_shape += [jax.ShapeDtypeStruct((rows, n_side), F32)]
        args += [w, b_side.reshape(1, n_side)]
    return pl.pallas_call(
        functools.partial(_norm_proj_kernel, has_side=b_side is not None, w_transposed=w_transposed),
        grid=(rows // tm, n_main // tn),
        in_specs=in_specs,
        out_specs=out_specs,
        out_shape=out_shape,
        scratch_shapes=[
            pltpu.VMEM((tm, d), BF16),
            pltpu.VMEM((WEIGHT_SLOTS, tn, d) if w_transposed else (WEIGHT_SLOTS, d, tn), w.dtype),
            pltpu.SemaphoreType.DMA((WEIGHT_SLOTS,)),
        ],
        compiler_params=_params("arbitrary", "arbitrary"),
        name="norm_proj",
    )(*args)


def _proj_residual_kernel(a_ref, w_ref, b_ref, h_ref, o_ref, wb_ref, *, tm, rows_per_batch):
    i = pl.program_id(1)

    @pl.when(i == 0)
    def _():
        wb_ref[...] = w_ref[...].astype(BF16)

    y = _dot(a_ref[...], wb_ref[...]) + b_ref[...]
    row = lax.rem(i * tm, rows_per_batch) + lax.broadcasted_iota(jnp.int32, (tm, 1), 0)
    o_ref[...] = h_ref[...] + jnp.where(_is_token_row(row, rows_per_batch - SEQ_BLOCK), y, 0.0)


PROJ_WEIGHT_TILE_BYTES = 16 * 1024 * 1024


def _proj_residual(a, w, bias, h, rows_per_batch, *, tm_target=528):
    rows, k = a.shape
    d = h.shape[1]
    tm = _divisor_tile(rows_per_batch, tm_target, 16)
    tn = _divisor_tile(d, PROJ_WEIGHT_TILE_BYTES // (4 * k), 2 * LANES)
    kern = functools.partial(_proj_residual_kernel, tm=tm, rows_per_batch=rows_per_batch)
    return pl.pallas_call(
        kern,
        grid=(d // tn, rows // tm),
        in_specs=[
            pl.BlockSpec((tm, k), lambda j, i: (i, 0)),
            pl.BlockSpec((k, tn), lambda j, i: (0, j), pipeline_mode=pl.Buffered(1)),
            pl.BlockSpec((1, tn), lambda j, i: (0, j)),
            pl.BlockSpec((tm, tn), lambda j, i: (i, j)),
        ],
        out_specs=pl.BlockSpec((tm, tn), lambda j, i: (i, j)),
        out_shape=jax.ShapeDtypeStruct((rows, d), F32),
        scratch_shapes=[pltpu.VMEM((k, tn), BF16)],
        compiler_params=_params("parallel", "arbitrary"),
        name="proj_residual",
    )(a, w, bias.reshape(1, d), h)


def _mlstm_kernel(q_ref, k_ref, v_ref, og_ref, g_ref, nw_ref, o_ref, c_ref, n_ref, m_ref):
    c = pl.program_id(1)
    blk = SEQ_BLOCK
    heads = range(MLSTM_HEADS)

    @pl.when(c == 0)
    def _():
        c_ref[...] = jnp.zeros_like(c_ref)
        n_ref[...] = jnp.zeros_like(n_ref)
        m_ref[...] = jnp.zeros_like(m_ref)

    gates = g_ref[...]
    row = c * blk + lax.broadcasted_iota(jnp.int32, (blk, 1), 0)
    valid = row >= LEAD_PAD
    log_f = jnp.where(valid, jnp.minimum(gates, 0.0) - jnp.log(1.0 + jnp.exp(-jnp.abs(gates))), 0.0)
    b_all = _cumsum_rows(log_f)
    li_all = jnp.where(valid, gates, -jnp.inf)

    ri = lax.broadcasted_iota(jnp.int32, (blk, blk), 0)
    ci = lax.broadcasted_iota(jnp.int32, (blk, blk), 1)
    eye = ri == ci
    causal = ci <= ri
    scale = MLSTM_DK ** -0.5

    q = {hd: q_ref[:, hd * MLSTM_DK:(hd + 1) * MLSTM_DK] for hd in heads}
    k = {hd: k_ref[:, hd * MLSTM_DK:(hd + 1) * MLSTM_DK] for hd in heads}
    v = {hd: v_ref[:, hd * MLSTM_DV:(hd + 1) * MLSTM_DV] for hd in heads}
    c_st = {hd: c_ref[hd] for hd in heads}
    n_st = {hd: n_ref[hd] for hd in heads}
    m_prev = {hd: m_ref[hd][:, :1] for hd in heads}

    qk_raw = {hd: _dot_nt(q[hd], k[hd]) for hd in heads}
    q_c = {hd: _dot(q[hd], c_st[hd].astype(BF16)) for hd in heads}

    b_c = {hd: b_all[:, MLSTM_HEADS + hd:MLSTM_HEADS + hd + 1] for hd in heads}
    d_c = {hd: li_all[:, hd:hd + 1] - b_c[hd] for hd in heads}
    b_last = {hd: b_c[hd][blk - 1:blk, :] for hd in heads}

    a_init, ka = {}, {}
    for hd in heads:
        log_end_init = b_last[hd] + m_prev[hd]
        log_end = b_last[hd] + d_c[hd]
        m_new = jnp.maximum(log_end_init, jnp.max(log_end, axis=0, keepdims=True))
        a_init[hd] = jnp.exp(log_end_init - m_new)
        ka[hd] = k[hd].astype(F32) * jnp.exp(log_end - m_new)
        m_ref[hd] = jnp.broadcast_to(m_new, (1, LANES))
    kv = {hd: _dot_tn(ka[hd].astype(BF16), v[hd]) for hd in heads}

    d_r = {hd: _column_to_row(d_c[hd], eye) for hd in heads}
    log_w = {hd: jnp.where(causal, b_c[hd] + d_r[hd], -jnp.inf) for hd in heads}
    log_init = {hd: b_c[hd] + m_prev[hd] for hd in heads}
    m_t = {hd: jnp.maximum(log_init[hd], jnp.max(log_w[hd], axis=-1, keepdims=True)) for hd in heads}
    w_init = {hd: jnp.exp(log_init[hd] - m_t[hd]) * scale for hd in heads}
    qk = {hd: qk_raw[hd] * (jnp.exp(log_w[hd] - m_t[hd]) * scale) for hd in heads}
    pv = {hd: _dot(qk[hd].astype(BF16), v[hd]) for hd in heads}
    qn = {hd: jnp.sum(q[hd].astype(F32) * n_st[hd], axis=-1, keepdims=True) for hd in heads}
    den = {hd: w_init[hd] * qn[hd] + jnp.sum(qk[hd], axis=-1, keepdims=True) for hd in heads}
    for hd in heads:
        hh = (w_init[hd] * q_c[hd] + pv[hd]) / jnp.maximum(jnp.abs(den[hd]), jnp.exp(-m_t[hd]))
        sl = slice(hd * MLSTM_DV, (hd + 1) * MLSTM_DV)
        gate = _sigmoid(og_ref[:, sl].astype(F32))
        o_ref[:, sl] = (_rms_normalize(hh) * nw_ref[:, sl] * gate).astype(o_ref.dtype)

    for hd in heads:
        c_ref[hd] = a_init[hd] * c_st[hd] + kv[hd]
        n_ref[hd] = a_init[hd] * n_st[hd] + jnp.sum(ka[hd], axis=0, keepdims=True)


def _mlstm_core(p_main, gates, norm_w, batch, rows_per_batch):
    blk = SEQ_BLOCK
    nblk = rows_per_batch // blk
    rows = p_main.shape[0]
    idx = lambda col: (lambda b, c: (_physical_block(b, c, nblk), col))
    return pl.pallas_call(
        _mlstm_kernel,
        grid=(batch, nblk),
        in_specs=[
            pl.BlockSpec((blk, MLSTM_QK_W), idx(0)),
            pl.BlockSpec((blk, MLSTM_QK_W), idx(1)),
            pl.BlockSpec((blk, MLSTM_V_W), idx(1)),
            pl.BlockSpec((blk, MLSTM_V_W), idx(2)),
            pl.BlockSpec((blk, 2 * MLSTM_HEADS), idx(0)),
            pl.BlockSpec((1, MLSTM_V_W), lambda b, c: (0, 0)),
        ],
        out_specs=pl.BlockSpec((blk, MLSTM_V_W), idx(0)),
        out_shape=jax.ShapeDtypeStruct((rows, MLSTM_V_W), BF16),
        scratch_shapes=[
            pltpu.VMEM((MLSTM_HEADS, MLSTM_DK, MLSTM_DV), F32),
            pltpu.VMEM((MLSTM_HEADS, 1, MLSTM_DK), F32),
            pltpu.VMEM((MLSTM_HEADS, 1, LANES), F32),
        ],
        compiler_params=_params("parallel", "arbitrary"),
        name="mlstm_core",
    )(p_main, p_main, p_main, p_main, gates, norm_w.reshape(1, MLSTM_V_W))


def _pool_kernel(h_ref, halo_ref, nw_ref, w_ref, sc_ref, o_ref, ext_ref, *, tm, rows_per_batch):
    i = pl.program_id(0)
    seq = rows_per_batch - SEQ_BLOCK
    x = h_ref[...]
    nw = nw_ref[...]
    u = _rms_normalize(x) * nw
    ext_ref[0:POOL_HALO, :] = _rms_normalize(halo_ref[...]) * nw
    ext_ref[POOL_HALO:, :] = u
    row = lax.rem(i * tm, rows_per_batch) + lax.broadcasted_iota(jnp.int32, (tm, 1), 0)
    pos = jnp.where(row < seq, row + N_META, row - (seq + LEAD_PAD))
    valid = pos >= 0
    for gi, win in enumerate(POOL_WINDOWS):
        sl = slice(gi * POOL_GROUP, (gi + 1) * POOL_GROUP)
        terms = [ext_ref[POOL_HALO - s:POOL_HALO - s + tm, sl] for s in range(win)]
        while len(terms) > 1:
            terms = [terms[a] + terms[a + 1] for a in range(0, len(terms), 2)]
        count = jnp.clip(pos + 1, 1, win).astype(F32)
        pooled = terms[0] / count - u[:, sl]
        y = _dot(pooled.astype(BF16), w_ref[gi].astype(BF16)) * sc_ref[:, sl]
        o_ref[:, sl] = x[:, sl] + jnp.where(valid, y, 0.0)


def _pool_mixer(h, nw, w_group, scale, rows_per_batch, *, tm_target=384):
    rows, d = h.shape
    tm = _divisor_tile(rows_per_batch, tm_target, POOL_HALO)
    tpb = rows_per_batch // tm
    halo_per_tile = tm // POOL_HALO
    halo_per_batch = rows_per_batch // POOL_HALO

    def halo_index(i):
        return jnp.where(i % tpb == 0, (i // tpb + 1) * halo_per_batch - 1, i * halo_per_tile - 1), 0

    kern = functools.partial(_pool_kernel, tm=tm, rows_per_batch=rows_per_batch)
    return pl.pallas_call(
        kern,
        grid=(rows // tm,),
        in_specs=[
            pl.BlockSpec((tm, d), lambda i: (i, 0)),
            pl.BlockSpec((POOL_HALO, d), halo_index),
            pl.BlockSpec((1, d), lambda i: (0, 0)),
            pl.BlockSpec(w_group.shape, lambda i: (0, 0, 0)),
            pl.BlockSpec((1, d), lambda i: (0, 0)),
        ],
        out_specs=pl.BlockSpec((tm, d), lambda i: (i, 0)),
        out_shape=jax.ShapeDtypeStruct((rows, d), F32),
        scratch_shapes=[pltpu.VMEM((tm + POOL_HALO, d), F32)],
        compiler_params=_params("parallel"),
        name="pool_mixer",
    )(h, h, nw.reshape(1, d), w_group, scale.reshape(1, d))


GDN_TAIL_GROUP = 16
GDN_CARRY = 16


def _unit_lower_inverses(strict_lowers):
    n = strict_lowers[0].shape[0]
    ri = lax.broadcasted_iota(jnp.int32, (n, n), 0)
    ci = lax.broadcasted_iota(jnp.int32, (n, n), 1)
    ident = jnp.where(ri == ci, 1.0, 0.0)
    invs = [ident - l for l in strict_lowers]
    powers_b = [l.astype(BF16) for l in strict_lowers]
    terms = 2
    while terms < n:
        powers_b = [_dot(ab, ab).astype(BF16) for ab in powers_b]
        invs = [p + _dot(p.astype(BF16), ab) for p, ab in zip(invs, powers_b)]
        terms *= 2
    return invs


def _gdn_chunk(r0, gc_all, beta_all, s_ref, q_s, k_s, v_s, o_s, head_group):
    ch = GDN_CHUNK
    rows = slice(r0, r0 + ch)
    ri = lax.broadcasted_iota(jnp.int32, (ch, ch), 0)
    ci = lax.broadcasted_iota(jnp.int32, (ch, ch), 1)
    eye = ri == ci
    causal = ci <= ri
    strict = ci < ri
    rep = GDN_V_HEADS // GDN_QK_HEADS
    for g0 in range(0, GDN_V_HEADS, head_group):
        vheads = range(g0, g0 + head_group)
        qheads = range(g0 // rep, (g0 + head_group) // rep)
        q = {j: q_s[j, rows, :] for j in qheads}
        k = {j: k_s[j, rows, :] for j in qheads}
        kk = {j: _dot_nt(k[j], k[j]) for j in qheads}
        qk = {j: _dot_nt(q[j], k[j]) for j in qheads}
        gc_c = {jv: gc_all[:, jv:jv + 1] for jv in vheads}
        beta_c = {jv: beta_all[:, jv:jv + 1] for jv in vheads}
        decay = {jv: jnp.exp(jnp.where(causal, gc_c[jv] - _column_to_row(gc_c[jv], eye), -jnp.inf)) for jv in vheads}
        t_inv = _unit_lower_inverses([jnp.where(strict, kk[jv // rep] * decay[jv], 0.0) * beta_c[jv] for jv in vheads])
        t_inv = dict(zip(vheads, t_inv))
        for t0 in range(g0, g0 + head_group, GDN_TAIL_GROUP):
            th = range(t0, min(t0 + GDN_TAIL_GROUP, g0 + head_group))
            egc = {jv: jnp.exp(gc_c[jv]) for jv in th}
            s_st = {jv: s_ref[jv] for jv in th}
            s_b = {jv: s_st[jv].astype(BF16) for jv in th}
            qs = {jv: _dot(q[jv // rep], s_b[jv]) * egc[jv] for jv in th}
            rhs = {jv: jnp.concatenate([v_s[jv, rows, :].astype(F32) * beta_c[jv],
                                        k[jv // rep].astype(F32) * (beta_c[jv] * egc[jv])], axis=-1).astype(BF16)
                   for jv in th}
            sol = {jv: _dot(t_inv[jv].astype(BF16), rhs[jv]) for jv in th}
            v_new = {jv: sol[jv][:, :GDN_DV] - _dot(sol[jv][:, GDN_DV:].astype(BF16), s_b[jv]) for jv in th}
            for jv in th:
                o_s[jv, rows, :] = qs[jv] + _dot((qk[jv // rep] * decay[jv]).astype(BF16), v_new[jv].astype(BF16))
            for jv in th:
                g_last = gc_c[jv][ch - 1:ch, :]
                v_dec = (v_new[jv] * jnp.exp(g_last - gc_c[jv])).astype(BF16)
                s_ref[jv] = jnp.exp(g_last) * s_st[jv] + _dot_tn(k[jv // rep], v_dec)


def _gdn_kernel(x_ref, z_ref, e_ref, cw_ref, al_ref, dt_ref, nw_ref, o_ref,
                ext_ref, s_ref, q_s, k_s, v_s, o_s, *, head_group):
    c = pl.program_id(1)
    blk = SEQ_BLOCK

    @pl.when(c == 0)
    def _():
        ext_ref[blk:, :] = jnp.zeros((GDN_CARRY, GDN_CONV_CH), BF16)
        s_ref[...] = jnp.zeros_like(s_ref)

    ext_ref[0:blk, :] = x_ref[...]
    n_shift = GDN_CONV - 1
    r = lax.broadcasted_iota(jnp.int32, (n_shift * blk, blk + GDN_CARRY), 0)
    src_col = lax.broadcasted_iota(jnp.int32, (n_shift * blk, blk + GDN_CARRY), 1)
    shift = 1 + (r >= blk).astype(jnp.int32) + (r >= 2 * blk).astype(jnp.int32)
    t = r - (shift - 1) * blk
    src = jnp.where(t >= shift, t - shift, blk + GDN_CARRY + t - shift)
    select_rows = jnp.where(src_col == src, 1.0, 0.0).astype(BF16)
    for cb2 in range(GDN_CONV_CH // (2 * LANES)):
        shifted = _dot(select_rows, ext_ref[:, cb2 * 2 * LANES:(cb2 + 1) * 2 * LANES])
        for half in range(2):
            cb = 2 * cb2 + half
            sl = slice(cb * LANES, (cb + 1) * LANES)
            acc = x_ref[:, sl].astype(F32) * cw_ref[n_shift:n_shift + 1, sl]
            for s in range(1, GDN_CONV):
                x_back = shifted[(s - 1) * blk:s * blk, half * LANES:(half + 1) * LANES]
                acc = acc + x_back * cw_ref[n_shift - s:n_shift - s + 1, sl]
            y = acc * _sigmoid(acc)
            if cb < 2 * GDN_QK_HEADS:
                y = y * lax.rsqrt(jnp.sum(y * y, axis=-1, keepdims=True) + EPS)
                if cb < GDN_QK_HEADS:
                    q_s[cb] = (y * (GDN_DK ** -0.5)).astype(BF16)
                else:
                    k_s[cb - GDN_QK_HEADS] = y.astype(BF16)
            else:
                v_s[cb - 2 * GDN_QK_HEADS] = y.astype(BF16)
    ext_ref[blk:, :] = x_ref[blk - GDN_CARRY:, :]

    row = c * blk + lax.broadcasted_iota(jnp.int32, (blk, 1), 0)
    valid = row >= LEAD_PAD
    e = e_ref[...]
    beta_all = jnp.where(valid, _sigmoid(e[:, :GDN_V_HEADS]), 0.0)
    g_all = jnp.where(valid, -jnp.exp(al_ref[...]) * _softplus(e[:, GDN_V_HEADS:] + dt_ref[...]), 0.0)

    for r0 in range(0, blk, GDN_CHUNK):
        rows = slice(r0, r0 + GDN_CHUNK)
        _gdn_chunk(r0, _cumsum_rows(g_all[rows]), beta_all[rows], s_ref, q_s, k_s, v_s, o_s, head_group)

    for jv in range(GDN_V_HEADS):
        sl = slice(jv * GDN_DV, (jv + 1) * GDN_DV)
        z = z_ref[:, sl].astype(F32)
        o_ref[:, sl] = (_rms_normalize(o_s[jv]) * nw_ref[...] * (z * _sigmoid(z))).astype(o_ref.dtype)


def _gdn_core(p_main, side, conv_w, a_log, dt_bias, norm_w, batch, rows_per_batch, *, head_group=32):
    blk = SEQ_BLOCK
    nblk = rows_per_batch // blk
    rows = p_main.shape[0]
    idx = lambda col: (lambda b, c: (_physical_block(b, c, nblk), col))
    const2 = lambda b, c: (0, 0)
    kern = functools.partial(_gdn_kernel, head_group=head_group)
    return pl.pallas_call(
        kern,
        grid=(batch, nblk),
        in_specs=[
            pl.BlockSpec((blk, GDN_CONV_CH), idx(0)),
            pl.BlockSpec((blk, GDN_V_W), idx(GDN_CONV_CH // GDN_V_W)),
            pl.BlockSpec((blk, 2 * GDN_V_HEADS), idx(0)),
            pl.BlockSpec((GDN_CONV, GDN_CONV_CH), const2),
            pl.BlockSpec((1, GDN_V_HEADS), const2),
            pl.BlockSpec((1, GDN_V_HEADS), const2),
            pl.BlockSpec((1, GDN_DV), const2),
        ],
        out_specs=pl.BlockSpec((blk, GDN_V_W), idx(0)),
        out_shape=jax.ShapeDtypeStruct((rows, GDN_V_W), BF16),
        scratch_shapes=[
            pltpu.VMEM((blk + GDN_CARRY, GDN_CONV_CH), BF16),
            pltpu.VMEM((GDN_V_HEADS, GDN_DK, GDN_DV), F32),
            pltpu.VMEM((GDN_QK_HEADS, blk, GDN_DK), BF16),
            pltpu.VMEM((GDN_QK_HEADS, blk, GDN_DK), BF16),
            pltpu.VMEM((GDN_V_HEADS, blk, GDN_DV), BF16),
            pltpu.VMEM((GDN_V_HEADS, blk, GDN_DV), F32),
        ],
        compiler_params=_params("parallel", "arbitrary"),
        name="gdn_core",
    )(p_main, p_main, side, conv_w, a_log.reshape(1, -1), dt_bias.reshape(1, -1), norm_w.reshape(1, -1))


def _rope_table_kernel(cos_ref, sin_ref):
    shape = cos_ref.shape
    pos = (lax.broadcasted_iota(jnp.int32, shape, 0) - LEAD_PAD).astype(F32)
    pair = lax.rem(lax.broadcasted_iota(jnp.int32, shape, 1), SWA_DH // 2).astype(F32)
    inv = jnp.exp(pair * (-2.0 * math.log(ROPE_THETA) / SWA_DH))
    ang = pos * inv
    cos_ref[...] = jnp.cos(ang)
    sin_ref[...] = jnp.sin(ang)


def _rope_tables(rows_per_batch):
    shape = jax.ShapeDtypeStruct((rows_per_batch, LANES), F32)
    return pl.pallas_call(_rope_table_kernel, out_shape=[shape, shape], name="rope_tables")()


SWA_COL_BATCH = 4


def _swa_kernel(sink_ref, q_ref, kv_ref, kvp_ref, cos_ref, sin_ref, cosp_ref, sinp_ref, o_ref):
    n = pl.program_id(1)
    blk = SEQ_BLOCK
    lane = lax.broadcasted_iota(jnp.int32, (1, LANES), 1)
    low = lane < SWA_DH
    first_half = lax.rem(lane, SWA_DH) < SWA_DH // 2

    def rope(x, cos, sin):
        rot = jnp.where(first_half, -pltpu.roll(x, LANES - SWA_DH // 2, 1), pltpu.roll(x, SWA_DH // 2, 1))
        return x * cos + rot * sin

    cos, sin = cos_ref[...], sin_ref[...]
    cosp, sinp = cosp_ref[...], sinp_ref[...]

    stacked = lax.broadcasted_iota(jnp.int32, (2 * blk, 2 * blk), 0)
    q_row = n * blk + jnp.where(stacked < blk, stacked, stacked - blk)
    k_row = (n - 1) * blk + lax.broadcasted_iota(jnp.int32, (2 * blk, 2 * blk), 1)
    mask = (k_row <= q_row) & (q_row - k_row < SWA_WINDOW) & (k_row >= LEAD_PAD)
    top = lax.broadcasted_iota(jnp.int32, (2 * blk, 1), 0) < blk

    def both_heads(x, head_in_pair):
        swapped = pltpu.roll(x, SWA_DH, 1)
        return jnp.where(low, x, swapped) if head_in_pair == 0 else jnp.where(low, swapped, x)

    kk, v_stack = {}, {}
    for g in range(SWA_HKV):
        pair_col, head_in_pair = divmod(g, 2)
        ksl = slice(pair_col * LANES, (pair_col + 1) * LANES)
        vsl = slice(SWA_KV_W + pair_col * LANES, SWA_KV_W + (pair_col + 1) * LANES)
        k_cur = both_heads(rope(kv_ref[:, ksl].astype(F32), cos, sin), head_in_pair)
        k_prev = both_heads(rope(kvp_ref[:, ksl].astype(F32), cosp, sinp), head_in_pair)
        kk[g] = jnp.concatenate([k_prev, k_cur], axis=0).astype(BF16)
        v_cur = both_heads(kv_ref[:, vsl].astype(F32), head_in_pair)
        v_prev = both_heads(kvp_ref[:, vsl].astype(F32), head_in_pair)
        v2 = jnp.concatenate([v_prev, v_cur], axis=0)
        v_stack[g] = jnp.concatenate([jnp.where(low, v2, 0.0), jnp.where(low, 0.0, v2)], axis=0).astype(BF16)

    def scores(col):
        qp = rope(q_ref[:, col * LANES:(col + 1) * LANES].astype(F32), cos, sin)
        q2 = jnp.concatenate([jnp.where(low, qp, 0.0), jnp.where(low, 0.0, qp)], axis=0).astype(BF16)
        s = _dot_nt(q2, kk[col // (SWA_GROUP // 2)]) * (SWA_DH ** -0.5)
        return jnp.where(mask, s, -jnp.inf)

    def attend(cols, s):
        sink = {c: jnp.where(top, sink_ref[2 * c], sink_ref[2 * c + 1]) for c in cols}
        m = {c: jnp.maximum(jnp.max(s[c], axis=-1, keepdims=True), sink[c]) for c in cols}
        e = {c: jnp.exp(s[c] - m[c]) for c in cols}
        inv = {c: 1.0 / (jnp.sum(e[c], axis=-1, keepdims=True) + jnp.exp(sink[c] - m[c])) for c in cols}
        for c in cols:
            prob = e[c] * inv[c]
            p2 = jnp.concatenate([prob[:blk], prob[blk:]], axis=1).astype(BF16)
            o_ref[:, c * LANES:(c + 1) * LANES] = _dot(p2, v_stack[c // (SWA_GROUP // 2)]).astype(o_ref.dtype)

    n_cols = SWA_HQ // 2
    batches = [range(c0, c0 + SWA_COL_BATCH) for c0 in range(0, n_cols, SWA_COL_BATCH)]
    pending = {c: scores(c) for c in batches[0]}
    for bi, cols in enumerate(batches):
        current = {c: pending.pop(c) for c in cols}
        if bi + 1 < len(batches):
            pending.update({c: scores(c) for c in batches[bi + 1]})
        attend(cols, current)


def _swa_core(p_main, sinks, cos, sin, batch, rows_per_batch):
    blk = SEQ_BLOCK
    nblk = rows_per_batch // blk
    rows = p_main.shape[0]
    kv_col = SWA_Q_W // (2 * SWA_KV_W)
    cur = lambda col: (lambda b, n: (_physical_block(b, n, nblk), col))
    prev = lambda col: (lambda b, n: (_physical_block(b, jnp.maximum(n - 1, 0), nblk), col))
    tab_cur = lambda b, n: (n, 0)
    tab_prev = lambda b, n: (jnp.maximum(n - 1, 0), 0)
    return pl.pallas_call(
        _swa_kernel,
        grid=(batch, nblk),
        in_specs=[
            pl.BlockSpec(memory_space=pltpu.SMEM),
            pl.BlockSpec((blk, SWA_Q_W), cur(0)),
            pl.BlockSpec((blk, 2 * SWA_KV_W), cur(kv_col)),
            pl.BlockSpec((blk, 2 * SWA_KV_W), prev(kv_col)),
            pl.BlockSpec((blk, LANES), tab_cur),
            pl.BlockSpec((blk, LANES), tab_cur),
            pl.BlockSpec((blk, LANES), tab_prev),
            pl.BlockSpec((blk, LANES), tab_prev),
        ],
        out_specs=pl.BlockSpec((blk, SWA_Q_W), cur(0)),
        out_shape=jax.ShapeDtypeStruct((rows, SWA_Q_W), BF16),
        compiler_params=_params("parallel", "parallel"),
        name="swa_core",
    )(sinks, p_main, p_main, p_main, cos, sin, cos, sin)


def kernel(x, meta_tokens, norm_w, ffn_w_gate, ffn_w_up, ffn_w_down, mlstm_w_in, mlstm_b_if, mlstm_norm_w, mlstm_w_out, pool_w, pool_scale, gdn_w_in, gdn_conv_w, gdn_a_log, gdn_dt_bias, gdn_norm_w, gdn_w_out, swa_w_qkv, swa_b_qkv, swa_sinks, swa_w_out, swa_b_out, final_norm_w):
    batch, seq, d = x.shape
    rpb = _rows_per_batch(seq)
    zeros_d = jnp.zeros((d,), F32)
    h = x
    for i in range(DEPTH):
        m, j = i % N_MIXERS, i // N_MIXERS
        h = _ffn(h, norm_w[i, 0], ffn_w_gate, ffn_w_up, ffn_w_down, i, 0, rpb,
                 meta=meta_tokens.astype(x.dtype) if i == 0 else None)
        h = h.reshape(batch * rpb, d)
        if m == 0:
            p_main, gates = _norm_proj(h, norm_w[i, 1], mlstm_w_in[j].T, jnp.zeros((MLSTM_MAIN_W,), F32),
                                       MLSTM_MAIN_W, mlstm_b_if[j], w_transposed=True)
            a = _mlstm_core(p_main, gates, mlstm_norm_w[j], batch, rpb)
            h = _proj_residual(a, mlstm_w_out[j], zeros_d, h, rpb)
        elif m == 1:
            h = _pool_mixer(h, norm_w[i, 1], pool_w[j], pool_scale[j], rpb)
        elif m == 2:
            p_main, side = _norm_proj(h, norm_w[i, 1], gdn_w_in[j].T, jnp.zeros((GDN_MAIN_W,), F32), GDN_MAIN_W,
                                      jnp.zeros((2 * GDN_V_HEADS,), F32), w_transposed=True)
            a = _gdn_core(p_main, side, gdn_conv_w[j], gdn_a_log[j], gdn_dt_bias[j], gdn_norm_w[j], batch, rpb)
            h = _proj_residual(a, gdn_w_out[j], zeros_d, h, rpb)
        else:
            (p_main,) = _norm_proj(h, norm_w[i, 1], swa_w_qkv[j], swa_b_qkv[j], SWA_IN)
            cos, sin = _rope_tables(rpb)
            a = _swa_core(p_main, swa_sinks[j], cos, sin, batch, rpb)
            h = _proj_residual(a, swa_w_out[j], swa_b_out[j], h, rpb)
        h = _ffn(h.reshape(batch, rpb, d), norm_w[i, 2], ffn_w_gate, ffn_w_up, ffn_w_down, i, 1, rpb,
                 final_w=final_norm_w if i == DEPTH - 1 else None)
    return h
```
